```python
import math
import jax, jax.numpy as jnp
from jax import lax
import numpy as np

D_MODEL = 2048
BATCH = 1
SEQ = 8192
DEPTH = 4

N_A_LAYERS = DEPTH // 2
N_B_LAYERS = DEPTH - N_A_LAYERS

MLA_HEADS = 16
Q_LORA = 512
KV_LORA = 512
QK_NOPE = 128
QK_ROPE = 64
V_HEAD = 128
ROPE_THETA = 10000.0
ATTN_Q_BLOCK = 128

MOBA_HEADS = 16
MOBA_HEAD = D_MODEL // MOBA_HEADS
MOBA_BLOCK = 256
MOBA_TOPK = 3
MOBA_Q_CHUNK = 32

REL_BUCKETS = 32
REL_MAX_DIST = 128

D_FF = 5632
CONV_WIDTH = 3

EPS = 1e-6
NEG = -1e30

kernel_name = 'hybrid_mla_moba_yoco'


def rms_norm(x, g):
    xf = x.astype(jnp.float32)
    y = xf * lax.rsqrt(jnp.mean(xf * xf, axis=-1, keepdims=True) + EPS)
    return (y * g.astype(jnp.float32)).astype(x.dtype)


def rope(x, pos):
    half = x.shape[-1] // 2
    inv = ROPE_THETA ** (-jnp.arange(half, dtype=jnp.float32) / half)
    ang = (pos.astype(jnp.float32)[..., None] * inv)[:, :, None, :]
    cos, sin = jnp.cos(ang), jnp.sin(ang)
    x1 = x[..., :half].astype(jnp.float32)
    x2 = x[..., half:].astype(jnp.float32)
    return jnp.concatenate([x1 * cos - x2 * sin, x1 * sin + x2 * cos], axis=-1).astype(x.dtype)


def rel_bucket(dist):
    n = jnp.maximum(dist, 0)
    max_exact = REL_BUCKETS // 2
    nf = jnp.maximum(n, 1).astype(jnp.float32)
    large = max_exact + (jnp.log(nf / max_exact) / math.log(REL_MAX_DIST / max_exact)
                         * (REL_BUCKETS - max_exact)).astype(jnp.int32)
    large = jnp.minimum(large, REL_BUCKETS - 1)
    return jnp.where(n < max_exact, n, large)


def causal_dense_attention(q, k, v, scale):
    B, S, H, Dq = q.shape
    nq = S // ATTN_Q_BLOCK
    qb = q.reshape(B, nq, ATTN_Q_BLOCK, H, Dq).transpose(1, 0, 2, 3, 4)
    k_idx = jnp.arange(S)

    def one_block(args):
        i, qblk = args
        logits = jnp.einsum('bqhd,bkhd->bhqk', qblk, k).astype(jnp.float32) * scale
        q_idx = i * ATTN_Q_BLOCK + jnp.arange(ATTN_Q_BLOCK)
        mask = k_idx[None, :] <= q_idx[:, None]
        logits = jnp.where(mask[None, None], logits, NEG)
        p = jax.nn.softmax(logits, axis=-1).astype(v.dtype)
        return jnp.einsum('bhqk,bkhd->bqhd', p, v)

    out = lax.map(one_block, (jnp.arange(nq), qb))
    return out.transpose(1, 0, 2, 3, 4).reshape(B, S, H, v.shape[-1])


def mla_mixer(xn, pos, w_in, q_norm_g, w_q_up, kv_norm_g, w_kv_up, w_o):
    B, S, _ = xn.shape
    h = xn @ w_in
    c_q, c_kv, k_r = jnp.split(h, [Q_LORA, Q_LORA + KV_LORA], axis=-1)
    q = (rms_norm(c_q, q_norm_g) @ w_q_up).reshape(B, S, MLA_HEADS, QK_NOPE + QK_ROPE)
    q = jnp.concatenate([q[..., :QK_NOPE], rope(q[..., QK_NOPE:], pos)], axis=-1)
    kv = (rms_norm(c_kv, kv_norm_g) @ w_kv_up).reshape(B, S, MLA_HEADS, QK_NOPE + V_HEAD)
    k_nope, v = kv[..., :QK_NOPE], kv[..., QK_NOPE:]
    k_r = rope(k_r[:, :, None, :], pos)
    k = jnp.concatenate([k_nope, jnp.broadcast_to(k_r, (B, S, MLA_HEADS, QK_ROPE))], axis=-1)
    o = causal_dense_attention(q, k, v, (QK_NOPE + QK_ROPE) ** -0.5)
    return o.reshape(B, S, MLA_HEADS * V_HEAD) @ w_o


def shared_moba_kv(h, pos, kv_norm_g, w_kv):
    B, S, _ = h.shape
    kv = rms_norm(h, kv_norm_g) @ w_kv
    k, v = jnp.split(kv, 2, axis=-1)
    nb = -(-S // MOBA_BLOCK)
    pad = nb * MOBA_BLOCK - S
    k = jnp.pad(k, ((0, 0), (0, pad), (0, 0))).reshape(B, nb, MOBA_BLOCK, MOBA_HEADS, MOBA_HEAD)
    v = jnp.pad(v, ((0, 0), (0, pad), (0, 0))).reshape(B, nb, MOBA_BLOCK, MOBA_HEADS, MOBA_HEAD)
    k_mean = jnp.mean(k.astype(jnp.float32), axis=2).astype(k.dtype)
    pos_kb = jnp.pad(pos, ((0, 0), (0, pad))).reshape(B, nb, MOBA_BLOCK)
    return k, v, k_mean, pos_kb


def moba_attention_seq(q, kb, vb, kmean, pos_q, pos_kb, bias_hb):
    S, H, Dk = q.shape
    NB = kb.shape[0]
    topk = min(MOBA_TOPK, NB)
    scale = Dk ** -0.5
    nchunk = S // MOBA_Q_CHUNK
    qc = q.reshape(nchunk, MOBA_Q_CHUNK, H, Dk)
    pc = pos_q.reshape(nchunk, MOBA_Q_CHUNK)
    kbT = kb.transpose(2, 0, 1, 3)
    vbT = vb.transpose(2, 0, 1, 3)
    head_ix = jnp.arange(H)
    blk_ix = jnp.arange(NB)

    def one_chunk(args):
        c, qq, pq = args
        t = c * MOBA_Q_CHUNK + jnp.arange(MOBA_Q_CHUNK)
        own = (c * MOBA_Q_CHUNK) // MOBA_BLOCK
        gate = jnp.einsum('qhd,nhd->qhn', qq, kmean).astype(jnp.float32)
        gate = jnp.where((blk_ix < own)[None, None, :], gate, -jnp.inf)
        _, sel = lax.top_k(gate, topk)
        valid = sel < own
        k_sel = kbT[head_ix[None, :, None], sel]
        v_sel = vbT[head_ix[None, :, None], sel]
        p_sel = pos_kb[sel]
        s_sel = jnp.einsum('qhd,qhnld->qhnl', qq, k_sel).astype(jnp.float32) * scale
        s_sel = s_sel + bias_hb[head_ix[None, :, None, None], rel_bucket(pq[:, None, None, None] - p_sel)]
        s_sel = jnp.where(valid[..., None], s_sel, NEG)
        k_own = lax.dynamic_index_in_dim(kb, own, 0, keepdims=False)
        v_own = lax.dynamic_index_in_dim(vb, own, 0, keepdims=False)
        p_own = lax.dynamic_index_in_dim(pos_kb, own, 0, keepdims=False)
        s_own = jnp.einsum('qhd,lhd->qhl', qq, k_own).astype(jnp.float32) * scale
        s_own = s_own + bias_hb[head_ix[None, :, None], rel_bucket(pq[:, None, None] - p_own[None, None, :])]
        causal = (own * MOBA_BLOCK + jnp.arange(MOBA_BLOCK))[None, :] <= t[:, None]
        s_own = jnp.where(causal[:, None, :], s_own, NEG)
        logits = jnp.concatenate([s_sel.reshape(MOBA_Q_CHUNK, H, topk * MOBA_BLOCK), s_own], axis=-1)
        p = jax.nn.softmax(logits, axis=-1).astype(vb.dtype)
        p_s = p[..., :topk * MOBA_BLOCK].reshape(MOBA_Q_CHUNK, H, topk, MOBA_BLOCK)
        p_o = p[..., topk * MOBA_BLOCK:]
        return (jnp.einsum('qhnl,qhnld->qhd', p_s, v_sel)
                + jnp.einsum('qhl,lhd->qhd', p_o, v_own))

    out = lax.map(one_chunk, (jnp.arange(nchunk), qc, pc))
    return out.reshape(S, H, vb.shape[-1])


def moba_mixer(xn, pos, w_q, w_o, rel_bias, kb, vb, kmean, pos_kb):
    B, S, _ = xn.shape
    q = (xn @ w_q).reshape(B, S, MOBA_HEADS, MOBA_HEAD)
    o = jax.vmap(moba_attention_seq, in_axes=(0, 0, 0, 0, 0, 0, None))(
        q, kb, vb, kmean, pos, pos_kb, rel_bias.T)
    return o.reshape(B, S, MOBA_HEADS * MOBA_HEAD) @ w_o


def conv_glu_ffn(xn, w_in, conv_w, conv_b, w_out):
    S = xn.shape[1]
    h = xn @ w_in
    hp = jnp.pad(h, ((0, 0), (CONV_WIDTH - 1, 0), (0, 0)))
    h = sum(hp[:, j:j + S] * conv_w[j] for j in range(CONV_WIDTH)) + conv_b
    gate, up = jnp.split(h, 2, axis=-1)
    return (jax.nn.gelu(gate, approximate=True) * up) @ w_out


def setup_inputs(seed: int = 0) -> dict:
    key = jax.random.key(seed)
    ks = jax.random.split(key, 20)
    f32 = jnp.float32

    def w(k, shape, fan_in):
        return jax.random.normal(k, shape, f32) * (fan_in ** -0.5)

    def gain(k, shape):
        return 1.0 + 0.02 * jax.random.normal(k, shape, f32)

    x = jax.random.normal(ks[0], (BATCH, SEQ, D_MODEL), f32)
    offs = jax.random.randint(ks[1], (BATCH, 1), 0, 1024, dtype=jnp.int32)
    positions = jnp.arange(SEQ, dtype=jnp.int32)[None, :] + offs
    return {
        'x': x,
        'positions': positions,
        'norm_gains': gain(ks[2], (DEPTH, 4, D_MODEL)),
        'a_w_in': w(ks[3], (N_A_LAYERS, D_MODEL, Q_LORA + KV_LORA + QK_ROPE), D_MODEL),
        'a_q_norm': gain(ks[4], (N_A_LAYERS, Q_LORA)),
        'a_w_q_up': w(ks[5], (N_A_LAYERS, Q_LORA, MLA_HEADS * (QK_NOPE + QK_ROPE)), Q_LORA),
        'a_kv_norm': gain(ks[6], (N_A_LAYERS, KV_LORA)),
        'a_w_kv_up': w(ks[7], (N_A_LAYERS, KV_LORA, MLA_HEADS * (QK_NOPE + V_HEAD)), KV_LORA),
        'a_w_o': w(ks[8], (N_A_LAYERS, MLA_HEADS * V_HEAD, D_MODEL), MLA_HEADS * V_HEAD),
        'b_kv_norm': gain(ks[9], (D_MODEL,)),
        'b_w_kv': w(ks[10], (D_MODEL, 2 * MOBA_HEADS * MOBA_HEAD), D_MODEL),
        'b_w_q': w(ks[11], (N_B_LAYERS, D_MODEL, MOBA_HEADS * MOBA_HEAD), D_MODEL),
        'b_w_o': w(ks[12], (N_B_LAYERS, MOBA_HEADS * MOBA_HEAD, D_MODEL), MOBA_HEADS * MOBA_HEAD),
        'rel_bias': 0.1 * jax.random.normal(ks[13], (REL_BUCKETS, MOBA_HEADS), f32),
        'ffn_w_in': w(ks[14], (DEPTH, D_MODEL, 2 * D_FF), D_MODEL),
        'ffn_conv_w': w(ks[15], (DEPTH, CONV_WIDTH, 2 * D_FF), CONV_WIDTH),
        'ffn_conv_b': 0.02 * jax.random.normal(ks[16], (DEPTH, 2 * D_FF), f32),
        'ffn_w_out': w(ks[17], (DEPTH, D_FF, D_MODEL), D_FF),
    }


def reference(x, positions, norm_gains, a_w_in, a_q_norm, a_w_q_up, a_kv_norm, a_w_kv_up, a_w_o,
              b_kv_norm, b_w_kv, b_w_q, b_w_o, rel_bias, ffn_w_in, ffn_conv_w, ffn_conv_b, ffn_w_out):
    h = x
    shared = None
    for layer in range(DEPTH):
        g = norm_gains[layer]
        if layer == N_A_LAYERS:
            shared = shared_moba_kv(h, positions, b_kv_norm, b_w_kv)
        xn = rms_norm(h, g[0])
        if layer < N_A_LAYERS:
            mix = mla_mixer(xn, positions, a_w_in[layer], a_q_norm[layer], a_w_q_up[layer],
                            a_kv_norm[layer], a_w_kv_up[layer], a_w_o[layer])
        else:
            j = layer - N_A_LAYERS
            kb, vb, kmean, pos_kb = shared
            mix = moba_mixer(xn, positions, b_w_q[j], b_w_o[j], rel_bias, kb, vb, kmean, pos_kb)
        h = h + rms_norm(mix, g[1])
        f = conv_glu_ffn(rms_norm(h, g[2]), ffn_w_in[layer], ffn_conv_w[layer],
                         ffn_conv_b[layer], ffn_w_out[layer])
        h = h + rms_norm(f, g[3])
    return h
```

```python
import functools
import math

import jax
import jax.numpy as jnp
from jax import lax
from jax.experimental import pallas as pl
from jax.experimental.pallas import tpu as pltpu

F32 = jnp.float32
BF16 = jnp.bfloat16

MLA_HEADS = 16
Q_LORA = 512
KV_LORA = 512
QK_NOPE = 128
QK_ROPE = 64
V_HEAD = 128
ROPE_THETA = 10000.0
MOBA_HEADS = 16
MOBA_HEAD = 128
MOBA_BLOCK = 256
MOBA_TOPK = 3
REL_BUCKETS = 32
REL_MAX_DIST = 128
CONV_WIDTH = 3
EPS = 1e-6
NEG = -1e30
LOG2E = 1.4426950408889634

LANES = 128
FAR_DIST = REL_MAX_DIST
VMEM_LIMIT = 56 * 1024 * 1024


def _params(*sem):
    return pltpu.CompilerParams(dimension_semantics=sem, vmem_limit_bytes=VMEM_LIMIT)


def _rms_scale(x):
    return lax.rsqrt(jnp.mean(x * x, axis=-1, keepdims=True) + EPS)


def _dot(a, b):
    return jnp.dot(a, b, preferred_element_type=F32)


def _dot_nt(a, b):
    return lax.dot_general(a, b, (((1,), (1,)), ((), ())), preferred_element_type=F32)


def _norm_kernel(x_ref, g_ref, o_ref):
    x = x_ref[...]
    o_ref[...] = (x * _rms_scale(x) * g_ref[...]).astype(o_ref.dtype)


def norm_cast(x, g, tm=512):
    S, D = x.shape
    tm = min(tm, S)
    return pl.pallas_call(
        _norm_kernel,
        grid=(S // tm,),
        in_specs=[pl.BlockSpec((tm, D), lambda i: (i, 0)), pl.BlockSpec((1, D), lambda i: (0, 0))],
        out_specs=pl.BlockSpec((tm, D), lambda i: (i, 0)),
        out_shape=jax.ShapeDtypeStruct((S, D), BF16),
        compiler_params=_params("parallel"),
        name="norm_cast",
    )(x, g.reshape(1, D))


def _rope_table_kernel(pos_ref, inv_ref, sgn_ref, cos_ref, sin_ref):
    ang = pos_ref[...].astype(F32) * inv_ref[...]
    cos_ref[...] = jnp.cos(ang)
    sin_ref[...] = jnp.sin(ang) * sgn_ref[...]


def rope_tables(pos_col, tm=1024):
    S = pos_col.shape[0]
    tm = min(tm, S)
    half = QK_ROPE // 2
    inv = ROPE_THETA ** (-jnp.arange(half, dtype=F32) / half)
    z = jnp.zeros((half,), F32)
    inv_pat = jnp.concatenate([inv, z, inv, z]).reshape(1, LANES)
    o = jnp.ones((2 * half,), F32)
    sgn = jnp.concatenate([-o, o]).reshape(1, LANES)
    return pl.pallas_call(
        _rope_table_kernel,
        grid=(S // tm,),
        in_specs=[pl.BlockSpec((tm, 1), lambda i: (i, 0)),
                  pl.BlockSpec((1, LANES), lambda i: (0, 0)),
                  pl.BlockSpec((1, LANES), lambda i: (0, 0))],
        out_specs=[pl.BlockSpec((tm, LANES), lambda i: (i, 0))] * 2,
        out_shape=[jax.ShapeDtypeStruct((S, LANES), F32)] * 2,
        compiler_params=_params("parallel"),
        name="rope_tables",
    )(pos_col, inv_pat, sgn)


def _mla_proj_kernel(xn_ref, w1_ref, gq_ref, gkv_ref, wqn_ref, wqr_ref, wkv_ref, cos_ref, sin_ref,
                     qn_ref, qr_ref, kn_ref, kr_ref, v_ref, *, q_scale):
    xn = xn_ref[...]
    cos = cos_ref[...]
    sin = sin_ref[...]

    def rope(x):
        return x * cos + pltpu.roll(x, LANES // 2, axis=1) * sin

    cq = _dot(xn, w1_ref[:, 0:Q_LORA])
    ckv = _dot(xn, w1_ref[:, Q_LORA:Q_LORA + KV_LORA])
    kr = _dot(xn, w1_ref[:, Q_LORA + KV_LORA:])
    kr_ref[...] = rope(kr).astype(BF16)
    cqn = (cq * _rms_scale(cq) * gq_ref[...]).astype(BF16)
    ckvn = (ckv * _rms_scale(ckv) * gkv_ref[...]).astype(BF16)
    width = MLA_HEADS * QK_NOPE
    chunk = 4 * LANES
    for c in range(width // chunk):
        sl = slice(c * chunk, (c + 1) * chunk)
        qn_ref[:, sl] = (_dot(cqn, wqn_ref[:, sl]) * q_scale).astype(BF16)
        qr = _dot(cqn, wqr_ref[:, sl])
        for hh in range(chunk // LANES):
            x = qr[:, hh * LANES:(hh + 1) * LANES]
            lo = c * chunk + hh * LANES
            qr_ref[:, lo:lo + LANES] = (rope(x) * q_scale).astype(BF16)
        kn_ref[:, sl] = _dot(ckvn, wkv_ref[:, sl]).astype(BF16)
        v_ref[:, sl] = _dot(ckvn, wkv_ref[:, width + c * chunk:width + (c + 1) * chunk]).astype(BF16)


def mla_proj(xn, w1, gq, gkv, wqn, wqr, wkv, cos_t, sin_t, q_scale, tm=256):
    S, D = xn.shape
    tm = min(tm, S)
    W = MLA_HEADS * QK_NOPE
    full = lambda a: pl.BlockSpec(a.shape, lambda i: (0, 0))
    row = lambda n: pl.BlockSpec((tm, n), lambda i: (i, 0))
    return pl.pallas_call(
        functools.partial(_mla_proj_kernel, q_scale=q_scale),
        grid=(S // tm,),
        in_specs=[row(D), full(w1), full(gq), full(gkv), full(wqn), full(wqr), full(wkv),
                  row(LANES), row(LANES)],
        out_specs=[row(W), row(W), row(W), row(LANES), row(W)],
        out_shape=[jax.ShapeDtypeStruct((S, W), BF16), jax.ShapeDtypeStruct((S, W), BF16),
                   jax.ShapeDtypeStruct((S, W), BF16), jax.ShapeDtypeStruct((S, LANES), BF16),
                   jax.ShapeDtypeStruct((S, W), BF16)],
        compiler_params=_params("parallel"),
        name="mla_proj",
    )(xn, w1, gq, gkv, wqn, wqr, wkv, cos_t, sin_t)


def _online_softmax_step(s, v, m_ref, l_ref, acc_ref):
    tk = s.shape[1]
    m_prev = m_ref[...]
    m_new = jnp.maximum(m_prev, jnp.max(s, axis=1, keepdims=True))
    alpha = jnp.exp2(m_prev - m_new)
    p = jnp.exp2(s - pltpu.repeat(m_new, tk // LANES, axis=1))
    l_ref[...] = alpha * l_ref[...] + jnp.sum(p, axis=1, keepdims=True)
    acc_ref[...] = alpha * acc_ref[...] + _dot(p.astype(BF16), v)
    m_ref[...] = m_new


def _mla_attn_kernel(qn_ref, qr_ref, kn_ref, kr_ref, v_ref, o_ref, m_ref, l_ref, acc_ref, *, tq):
    i = pl.program_id(1)
    q = jnp.concatenate([qn_ref[...], qr_ref[...]], axis=1)
    m_ref[...] = jnp.full(m_ref.shape, NEG, F32)
    l_ref[...] = jnp.zeros(l_ref.shape, F32)
    acc_ref[...] = jnp.zeros(acc_ref.shape, F32)

    def scores(j):
        rows = pl.ds(pl.multiple_of(j * tq, tq), tq)
        k = jnp.concatenate([kn_ref[rows, :], kr_ref[rows, :]], axis=1)
        return _dot_nt(q, k), v_ref[rows, :]

    def body(j, carry):
        s, v = scores(j)
        _online_softmax_step(s, v, m_ref, l_ref, acc_ref)
        return carry

    lax.fori_loop(0, i, body, 0)
    s, v = scores(i)
    r = lax.broadcasted_iota(jnp.int32, s.shape, 0)
    c = lax.broadcasted_iota(jnp.int32, s.shape, 1)
    _online_softmax_step(jnp.where(c <= r, s, NEG), v, m_ref, l_ref, acc_ref)
    o_ref[...] = (acc_ref[...] / l_ref[...]).astype(o_ref.dtype)


def mla_attention(qn, qr, kn, kr, v, tq=512):
    S, W = qn.shape
    tq = min(tq, S)
    H = W // LANES
    qspec = pl.BlockSpec((tq, LANES), lambda h, i: (i, h))
    kspec = pl.BlockSpec((S, LANES), lambda h, i: (0, h))
    return pl.pallas_call(
        functools.partial(_mla_attn_kernel, tq=tq),
        grid=(H, S // tq),
        in_specs=[qspec, qspec, kspec, pl.BlockSpec((S, LANES), lambda h, i: (0, 0)), kspec],
        out_specs=qspec,
        out_shape=jax.ShapeDtypeStruct((S, W), BF16),
        scratch_shapes=[pltpu.VMEM((tq, LANES), F32)] * 3,
        compiler_params=_params("parallel", "arbitrary"),
        name="mla_attention",
    )(qn, qr, kn, kr, v)


def _proj_res_kernel(o_ref, w_ref, h_ref, gpost_ref, gnext_ref, hn_ref, xn_ref, mix_ref):
    o = o_ref[...]
    n = w_ref.shape[1]
    chunk = 4 * LANES
    for c in range(n // chunk):
        sl = slice(c * chunk, (c + 1) * chunk)
        mix_ref[:, sl] = _dot(o, w_ref[:, sl])
    mix = mix_ref[...]
    hn = h_ref[...] + mix * _rms_scale(mix) * gpost_ref[...]
    hn_ref[...] = hn
    xn_ref[...] = (hn * _rms_scale(hn) * gnext_ref[...]).astype(xn_ref.dtype)


def proj_res_norm(o, w, h, g_post, g_next, tm=256):
    S, K = o.shape
    D = w.shape[1]
    tm = min(tm, S)
    row = lambda n: pl.BlockSpec((tm, n), lambda i: (i, 0))
    gspec = pl.BlockSpec((1, D), lambda i: (0, 0))
    return pl.pallas_call(
        _proj_res_kernel,
        grid=(S // tm,),
        in_specs=[row(K), pl.BlockSpec((K, D), lambda i: (0, 0)), row(D), gspec, gspec],
        out_specs=[row(D), row(D)],
        out_shape=[jax.ShapeDtypeStruct((S, D), F32), jax.ShapeDtypeStruct((S, D), BF16)],
        scratch_shapes=[pltpu.VMEM((tm, D), F32)],
        compiler_params=_params("parallel"),
        name="proj_res_norm",
    )(o, w, h, g_post.reshape(1, D), g_next.reshape(1, D))


HALO = 16


def _gelu_tanh(x):
    return 0.5 * x * (1.0 + jnp.tanh(math.sqrt(2.0 / math.pi) * (x + 0.044715 * (x * x * x))))


def _ffn_kernel(xn_ref, halo_ref, wg_ref, wu_ref, cwg_ref, cwu_ref, cbg_ref, cbu_ref, wo_ref, h_ref,
                gpost_ref, gnext_ref, hn_ref, *rest, n_next):
    xn_next_refs = rest[:n_next]
    xcat_ref, acc_ref = rest[n_next:]
    i = pl.program_id(0)
    c = pl.program_id(1)
    tm = xn_ref.shape[0]

    @pl.when(c == 0)
    def _():
        halo = halo_ref[...]
        xcat_ref[0:HALO, :] = jnp.where(i == 0, jnp.zeros_like(halo), halo)
        xcat_ref[HALO:, :] = xn_ref[...]
        acc_ref[...] = jnp.zeros(acc_ref.shape, F32)

    xcat = xcat_ref[...]

    def conv(w_ref, cw_ref, cb_ref):
        y = _dot(xcat, w_ref[...])
        cw = cw_ref[...]
        out = (y[HALO - 2:HALO - 2 + tm] * cw[0:1] + y[HALO - 1:HALO - 1 + tm] * cw[1:2]
               + y[HALO:] * cw[2:3])
        return out + cb_ref[...]

    act = _gelu_tanh(conv(wg_ref, cwg_ref, cbg_ref)) * conv(wu_ref, cwu_ref, cbu_ref)
    acc_ref[...] += _dot(act.astype(BF16), wo_ref[...])

    @pl.when(c == pl.num_programs(1) - 1)
    def _():
        f = acc_ref[...]
        hn = h_ref[...] + f * _rms_scale(f) * gpost_ref[...]
        hn_ref[...] = hn
        if n_next:
            y = hn * _rms_scale(hn)
            for k in range(n_next):
                xn_next_refs[k][...] = (y * gnext_ref[k:k + 1, :]).astype(BF16)


def ffn(xn, w_in, conv_w, conv_b, w_out, h, g_post, g_next, tm=512, tf=512):
    S, D = xn.shape
    FF = w_out.shape[0]
    tm = min(tm, S)
    nf = FF // tf
    n_next = 0 if g_next is None else g_next.shape[0]
    if g_next is None:
        g_next = jnp.ones((1, D), F32)
    cb = conv_b.reshape(1, 2 * FF)
    hb = tm // HALO
    row = pl.BlockSpec((tm, D), lambda i, c: (i, 0))
    in_specs = [
        row,
        pl.BlockSpec((HALO, D), lambda i, c: (jnp.maximum(i * hb - 1, 0), 0)),
        pl.BlockSpec((D, tf), lambda i, c: (0, c)),
        pl.BlockSpec((D, tf), lambda i, c: (0, c + nf)),
        pl.BlockSpec((CONV_WIDTH, tf), lambda i, c: (0, c)),
        pl.BlockSpec((CONV_WIDTH, tf), lambda i, c: (0, c + nf)),
        pl.BlockSpec((1, tf), lambda i, c: (0, c)),
        pl.BlockSpec((1, tf), lambda i, c: (0, c + nf)),
        pl.BlockSpec((tf, D), lambda i, c: (c, 0)),
        row,
        pl.BlockSpec((1, D), lambda i, c: (0, 0)),
        pl.BlockSpec(g_next.shape, lambda i, c: (0, 0)),
    ]
    outs = pl.pallas_call(
        functools.partial(_ffn_kernel, n_next=n_next),
        grid=(S // tm, nf),
        in_specs=in_specs,
        out_specs=[row] * (1 + n_next),
        out_shape=[jax.ShapeDtypeStruct((S, D), F32)] + [jax.ShapeDtypeStruct((S, D), BF16)] * n_next,
        scratch_shapes=[pltpu.VMEM((HALO + tm, D), BF16), pltpu.VMEM((tm, D), F32)],
        compiler_params=_params("parallel", "arbitrary"),
        name="conv_glu_ffn",
    )(xn, xn, w_in, w_in, conv_w, conv_w, cb, cb, w_out, h, g_post.reshape(1, D), g_next)
    return outs[0], list(outs[1:])


def _matmul_kernel(x_ref, w_ref, o_ref, *mean_ref, scale):
    y = _dot(x_ref[...], w_ref[...])
    if scale != 1.0:
        y = y * scale
    o_ref[...] = y.astype(o_ref.dtype)
    if mean_ref:
        mean_ref[0][...] = jnp.mean(y, axis=0, keepdims=True)[None]


def matmul(x, w, scale=1.0, tm=512, tn=1024, with_mean=False):
    S, K = x.shape
    N = w.shape[1]
    tm = min(tm, S)
    out_specs = [pl.BlockSpec((tm, tn), lambda n, i: (i, n))]
    out_shape = [jax.ShapeDtypeStruct((S, N), BF16)]
    if with_mean:
        out_specs.append(pl.BlockSpec((1, 1, tn), lambda n, i: (i, 0, n)))
        out_shape.append(jax.ShapeDtypeStruct((S // tm, 1, N), F32))
    outs = pl.pallas_call(
        functools.partial(_matmul_kernel, scale=scale),
        grid=(N // tn, S // tm),
        in_specs=[pl.BlockSpec((tm, K), lambda n, i: (i, 0)), pl.BlockSpec((K, tn), lambda n, i: (0, n))],
        out_specs=out_specs,
        out_shape=out_shape,
        compiler_params=_params("parallel", "parallel"),
        name="matmul_mean" if with_mean else "matmul",
    )(x, w)
    return outs if with_mean else outs[0]


def _rel_bucket(dist):
    n = jnp.maximum(dist, 0)
    max_exact = REL_BUCKETS // 2
    nf = jnp.maximum(n, 1).astype(F32)
    large = max_exact + (jnp.log(nf / max_exact) / math.log(REL_MAX_DIST / max_exact)
                         * (REL_BUCKETS - max_exact)).astype(jnp.int32)
    large = jnp.minimum(large, REL_BUCKETS - 1)
    return jnp.where(n < max_exact, n, large)


def _moba_kernel(qmin_ref, kmax_ref, q_ref, k_ref, v_ref, kmean_ref, posq_ref, posk_ref, tbl_ref,
                 o_ref, qaug_ref, m_ref, l_ref, acc_ref):
    h = pl.program_id(0)
    i = pl.program_id(1)
    L = MOBA_BLOCK
    q = q_ref[...]

    nbp = min(LANES, -(-posk_ref.shape[0] // 8) * 8)
    g = _dot_nt(kmean_ref[...].astype(BF16), q)[:nbp]
    blk = lax.broadcasted_iota(jnp.int32, g.shape, 0)
    g = jnp.where(blk < i, g, -jnp.inf)
    rank = jnp.zeros(g.shape, F32)
    for kk in range(nbp):
        row = g[kk:kk + 1, :]
        tie = jnp.where(blk > kk, 1.0, 0.0)
        rank = rank + jnp.where(row > g, 1.0, jnp.where(row == g, tie, 0.0))
    rank = jnp.where(blk < i, rank, jnp.where(blk == i, 0.0, float(MOBA_TOPK)))
    pen = jnp.where(rank < MOBA_TOPK, 0.0, NEG)
    pen = jnp.concatenate([pen, jnp.zeros((LANES - nbp, L), F32)], axis=0) if nbp < LANES else pen
    qaug_ref[:, 0:LANES] = q
    qaug_ref[:, LANES:] = pen.T.astype(BF16)
    qaug = qaug_ref[...]

    m_ref[...] = jnp.full(m_ref.shape, NEG, F32)
    l_ref[...] = jnp.zeros(l_ref.shape, F32)
    acc_ref[...] = jnp.zeros(acc_ref.shape, F32)

    lane = lax.broadcasted_iota(jnp.int32, (L, LANES), 1)
    posq = posq_ref[...]
    tbl = jnp.broadcast_to(tbl_ref[pl.ds(h, 1), :], (L, LANES))

    def scores(j):
        rows = pl.ds(pl.multiple_of(j * L, L), L)
        onehot = jnp.where(lane == j, 1.0, 0.0).astype(BF16)
        kaug = jnp.concatenate([k_ref[rows, :], onehot], axis=1)
        return _dot_nt(qaug, kaug), v_ref[rows, :]

    def bias(j):
        bucket = _rel_bucket(posq - posk_ref[pl.ds(j, 1), :])
        halves = [jnp.take_along_axis(tbl, bucket[:, t * LANES:(t + 1) * LANES], axis=1)
                  for t in range(L // LANES)]
        return jnp.concatenate(halves, axis=1)

    def body(j, carry):
        s, v = scores(j)
        near = qmin_ref[i] - kmax_ref[j] < FAR_DIST
        s = lax.cond(near, lambda: s + bias(j), lambda: s)
        _online_softmax_step(s, v, m_ref, l_ref, acc_ref)
        return carry

    lax.fori_loop(0, i, body, 0)
    s, v = scores(i)
    r = lax.broadcasted_iota(jnp.int32, s.shape, 0)
    c = lax.broadcasted_iota(jnp.int32, s.shape, 1)
    s = jnp.where(c <= r, s + bias(i), NEG)
    _online_softmax_step(s, v, m_ref, l_ref, acc_ref)
    o_ref[...] = (acc_ref[...] / l_ref[...]).astype(o_ref.dtype)


def moba_attention(q, kv, kmean_p, pos_col, pos_blk, tbl, qmin, kmax):
    S, W = q.shape
    H = W // LANES
    L = MOBA_BLOCK
    NB = S // L
    qspec = pl.BlockSpec((L, LANES), lambda h, i, *_: (i, h))
    grid_spec = pltpu.PrefetchScalarGridSpec(
        num_scalar_prefetch=2,
        grid=(H, NB),
        in_specs=[
            qspec,
            pl.BlockSpec((S, LANES), lambda h, i, *_: (0, h)),
            pl.BlockSpec((S, LANES), lambda h, i, *_: (0, H + h)),
            pl.BlockSpec((kmean_p.shape[0], LANES), lambda h, i, *_: (0, h)),
            pl.BlockSpec((L, 1), lambda h, i, *_: (i, 0)),
            pl.BlockSpec(pos_blk.shape, lambda h, i, *_: (0, 0)),
            pl.BlockSpec(tbl.shape, lambda h, i, *_: (0, 0)),
        ],
        out_specs=qspec,
        scratch_shapes=[pltpu.VMEM((L, 2 * LANES), BF16)] + [pltpu.VMEM((L, LANES), F32)] * 3,
    )
    return pl.pallas_call(
        _moba_kernel,
        grid_spec=grid_spec,
        out_shape=jax.ShapeDtypeStruct((S, W), BF16),
        compiler_params=_params("parallel", "arbitrary"),
        name="moba_attention",
    )(qmin, kmax, q, kv, kv, kmean_p, pos_col, pos_blk, tbl)


def _rope_lanes(w):
    half = QK_ROPE // 2
    z = jnp.zeros(w.shape[:-1] + (half,), w.dtype)
    return jnp.concatenate([w[..., :half], z, w[..., half:], z], axis=-1)


def _prep_mla(w_in, w_q_up, w_kv_up):
    D = w_in.shape[0]
    w1 = jnp.concatenate([w_in[:, :Q_LORA + KV_LORA], _rope_lanes(w_in[:, Q_LORA + KV_LORA:])], axis=1)
    wq = w_q_up.reshape(Q_LORA, MLA_HEADS, QK_NOPE + QK_ROPE)
    wqn = wq[:, :, :QK_NOPE].reshape(Q_LORA, MLA_HEADS * QK_NOPE)
    wqr = _rope_lanes(wq[:, :, QK_NOPE:]).reshape(Q_LORA, MLA_HEADS * LANES)
    wkv = w_kv_up.reshape(KV_LORA, MLA_HEADS, QK_NOPE + V_HEAD)
    wkv = jnp.concatenate([wkv[:, :, :QK_NOPE].reshape(KV_LORA, -1), wkv[:, :, QK_NOPE:].reshape(KV_LORA, -1)],
                          axis=1)
    return w1.astype(BF16), wqn.astype(BF16), wqr.astype(BF16), wkv.astype(BF16)


def kernel(x, positions, norm_gains, a_w_in, a_q_norm, a_w_q_up, a_kv_norm, a_w_kv_up, a_w_o, b_kv_norm, b_w_kv,
           b_w_q, b_w_o, rel_bias, ffn_w_in, ffn_conv_w, ffn_conv_b, ffn_w_out):
    B, S, D = x.shape
    depth = norm_gains.shape[0]
    n_a = a_w_in.shape[0]
    L = MOBA_BLOCK
    NB = S // L
    outs = []
    for b in range(B):
        pos = positions[b]
        pos_col = pos.reshape(S, 1)
        pos_blk = pos.reshape(NB, L)
        qmin = jnp.min(pos_blk, axis=1)
        kmax = jnp.max(pos_blk, axis=1)
        cos_t, sin_t = rope_tables(pos_col)
        tbl = (rel_bias - rel_bias[REL_BUCKETS - 1:REL_BUCKETS, :]).T * LOG2E
        tbl = jnp.pad(tbl, ((0, 0), (0, LANES - REL_BUCKETS)))

        h = x[b]
        xn = norm_cast(h, norm_gains[0, 0])
        kv = kmean_p = None
        for layer in range(depth):
            g = norm_gains[layer]
            if layer < n_a:
                w1, wqn, wqr, wkv = _prep_mla(a_w_in[layer], a_w_q_up[layer], a_w_kv_up[layer])
                q_scale = (QK_NOPE + QK_ROPE) ** -0.5 * LOG2E
                qn, qr, kn, kr, v = mla_proj(xn, w1, a_q_norm[layer].reshape(1, -1), a_kv_norm[layer].reshape(1, -1),
                                             wqn, wqr, wkv, cos_t, sin_t, q_scale)
                o = mla_attention(qn, qr, kn, kr, v)
                w_o = a_w_o[layer]
            else:
                j = layer - n_a
                q = matmul(xn, b_w_q[j].astype(BF16), scale=MOBA_HEAD ** -0.5 * LOG2E)
                o = moba_attention(q, kv, kmean_p, pos_col, pos_blk, tbl, qmin, kmax)
                w_o = b_w_o[j]
            h, xn = proj_res_norm(o, w_o.astype(BF16), h, g[1], g[2])
            if layer + 1 == depth:
                g_next = None
            elif layer + 1 == n_a:
                g_next = jnp.stack([norm_gains[layer + 1, 0], b_kv_norm])
            else:
                g_next = norm_gains[layer + 1, 0].reshape(1, D)
            h, nxt = ffn(xn, ffn_w_in[layer].astype(BF16), ffn_conv_w[layer], ffn_conv_b[layer],
                         ffn_w_out[layer].astype(BF16), h, g[3], g_next)
            if nxt:
                xn = nxt[0]
            if layer + 1 == n_a:
                kv, kmean = matmul(nxt[1], b_w_kv.astype(BF16), tm=L, with_mean=True)
                kmean_p = jnp.pad(kmean.reshape(NB, -1)[:, :MOBA_HEADS * MOBA_HEAD], ((0, LANES - NB), (0, 0)))
        outs.append(h)
    return jnp.stack(outs)
```

```python
import functools
import math

import jax
import jax.numpy as jnp
from jax import lax
from jax.experimental import pallas as pl
from jax.experimental.pallas import tpu as pltpu

F32 = jnp.float32
BF16 = jnp.bfloat16

MLA_HEADS = 16
Q_LORA = 512
KV_LORA = 512
QK_NOPE = 128
QK_ROPE = 64
V_HEAD = 128
ROPE_THETA = 10000.0
MOBA_HEADS = 16
MOBA_HEAD = 128
MOBA_BLOCK = 256
MOBA_TOPK = 3
REL_BUCKETS = 32
REL_MAX_DIST = 128
CONV_WIDTH = 3
EPS = 1e-6
NEG = -1e30
LOG2E = 1.4426950408889634

LANES = 128
FAR_DIST = REL_MAX_DIST
VMEM_LIMIT = 56 * 1024 * 1024


def _params(*sem):
    return pltpu.CompilerParams(dimension_semantics=sem, vmem_limit_bytes=VMEM_LIMIT)


def _rms_scale(x):
    return lax.rsqrt(jnp.mean(x * x, axis=-1, keepdims=True) + EPS)


def _dot(a, b):
    return jnp.dot(a, b, preferred_element_type=F32)


def _dot_nt(a, b):
    return lax.dot_general(a, b, (((1,), (1,)), ((), ())), preferred_element_type=F32)


def _norm_kernel(x_ref, g_ref, o_ref):
    x = x_ref[...]
    o_ref[...] = (x * _rms_scale(x) * g_ref[...]).astype(o_ref.dtype)


def norm_cast(x, g, tm=512):
    S, D = x.shape
    tm = min(tm, S)
    return pl.pallas_call(
        _norm_kernel,
        grid=(S // tm,),
        in_specs=[pl.BlockSpec((tm, D), lambda i: (i, 0)), pl.BlockSpec((1, D), lambda i: (0, 0))],
        out_specs=pl.BlockSpec((tm, D), lambda i: (i, 0)),
        out_shape=jax.ShapeDtypeStruct((S, D), BF16),
        compiler_params=_params("parallel"),
        name="norm_cast",
    )(x, g.reshape(1, D))


def _rope_table_kernel(pos_ref, inv_ref, sgn_ref, cos_ref, sin_ref):
    ang = pos_ref[...].astype(F32) * inv_ref[...]
    cos_ref[...] = jnp.cos(ang)
    sin_ref[...] = jnp.sin(ang) * sgn_ref[...]


def rope_tables(pos_col, tm=1024):
    S = pos_col.shape[0]
    tm = min(tm, S)
    half = QK_ROPE // 2
    inv = ROPE_THETA ** (-jnp.arange(half, dtype=F32) / half)
    z = jnp.zeros((half,), F32)
    inv_pat = jnp.concatenate([inv, z, inv, z]).reshape(1, LANES)
    o = jnp.ones((2 * half,), F32)
    sgn = jnp.concatenate([-o, o]).reshape(1, LANES)
    return pl.pallas_call(
        _rope_table_kernel,
        grid=(S // tm,),
        in_specs=[pl.BlockSpec((tm, 1), lambda i: (i, 0)),
                  pl.BlockSpec((1, LANES), lambda i: (0, 0)),
                  pl.BlockSpec((1, LANES), lambda i: (0, 0))],
        out_specs=[pl.BlockSpec((tm, LANES), lambda i: (i, 0))] * 2,
        out_shape=[jax.ShapeDtypeStruct((S, LANES), F32)] * 2,
        compiler_params=_params("parallel"),
        name="rope_tables",
    )(pos_col, inv_pat, sgn)


def _mla_proj_kernel(xn_ref, w1_ref, gq_ref, gkv_ref, wqn_ref, wqr_ref, wkn_ref, wvt_ref, cos_ref, sin_ref,
                     qn_ref, qr_ref, kn_ref, kr_ref, vt_ref, *, q_scale):
    xn = xn_ref[...]
    cos = cos_ref[...]
    sin = sin_ref[...]

    def rope(x):
        return x * cos + pltpu.roll(x, LANES // 2, axis=1) * sin

    cq = _dot(xn, w1_ref[:, 0:Q_LORA])
    ckv = _dot(xn, w1_ref[:, Q_LORA:Q_LORA + KV_LORA])
    kr = _dot(xn, w1_ref[:, Q_LORA + KV_LORA:])
    kr_ref[...] = rope(kr).astype(BF16)
    cqn = (cq * _rms_scale(cq) * gq_ref[...]).astype(BF16)
    ckvn = (ckv * _rms_scale(ckv) * gkv_ref[...]).astype(BF16)
    width = MLA_HEADS * QK_NOPE
    chunk = 4 * LANES
    for c in range(width // chunk):
        sl = slice(c * chunk, (c + 1) * chunk)
        qn_ref[:, sl] = (_dot(cqn, wqn_ref[:, sl]) * q_scale).astype(BF16)
        qr = _dot(cqn, wqr_ref[:, sl])
        for hh in range(chunk // LANES):
            x = qr[:, hh * LANES:(hh + 1) * LANES]
            lo = c * chunk + hh * LANES
            qr_ref[:, lo:lo + LANES] = (rope(x) * q_scale).astype(BF16)
        kn_ref[:, sl] = _dot(ckvn, wkn_ref[:, sl]).astype(BF16)
        vt_ref[sl, :] = _dot_nt(wvt_ref[sl, :], ckvn).astype(BF16)


def mla_proj(xn, w1, gq, gkv, wqn, wqr, wkn, wvt, cos_t, sin_t, q_scale, tm=256):
    S, D = xn.shape
    tm = min(tm, S)
    W = MLA_HEADS * QK_NOPE
    full = lambda a: pl.BlockSpec(a.shape, lambda i: (0, 0))
    row = lambda n: pl.BlockSpec((tm, n), lambda i: (i, 0))
    return pl.pallas_call(
        functools.partial(_mla_proj_kernel, q_scale=q_scale),
        grid=(S // tm,),
        in_specs=[row(D), full(w1), full(gq), full(gkv), full(wqn), full(wqr), full(wkn), full(wvt),
                  row(LANES), row(LANES)],
        out_specs=[row(W), row(W), row(W), row(LANES), pl.BlockSpec((W, tm), lambda i: (0, i))],
        out_shape=[jax.ShapeDtypeStruct((S, W), BF16), jax.ShapeDtypeStruct((S, W), BF16),
                   jax.ShapeDtypeStruct((S, W), BF16), jax.ShapeDtypeStruct((S, LANES), BF16),
                   jax.ShapeDtypeStruct((W, S), BF16)],
        compiler_params=_params("parallel"),
        name="mla_proj",
    )(xn, w1, gq, gkv, wqn, wqr, wkn, wvt, cos_t, sin_t)


def _flash_step_t(st, vt, m_ref, l_ref, acc_ref, g):
    tk, tq = st.shape
    m_prev = m_ref[g]
    m_new = jnp.maximum(m_prev, jnp.max(st, axis=0, keepdims=True))
    alpha = jnp.exp2(m_prev - m_new)
    p = jnp.exp2(st - m_new)
    l_ref[g] = alpha * l_ref[g] + jnp.sum(p.reshape(tk // 8, 8, tq), axis=0)
    acc_ref[g] = alpha * acc_ref[g] + _dot(vt, p.astype(BF16))
    m_ref[g] = m_new


def _flash_init(m_ref, l_ref, acc_ref, g):
    m_ref[g] = jnp.full(m_ref.shape[1:], NEG, F32)
    l_ref[g] = jnp.zeros(l_ref.shape[1:], F32)
    acc_ref[g] = jnp.zeros(acc_ref.shape[1:], F32)


def _flash_out(l_ref, acc_ref, g):
    l = jnp.sum(l_ref[g], axis=0, keepdims=True)
    return (acc_ref[g] / l).T


def _mla_attn_kernel(qn_ref, qr_ref, kn_ref, kr_ref, vt_ref, o_ref, q_ref, m_ref, l_ref, acc_ref, *, tq, heads):
    i = pl.program_id(1)
    for g in range(heads):
        hs = slice(g * LANES, (g + 1) * LANES)
        q_ref[g, :, 0:LANES] = qn_ref[:, hs]
        q_ref[g, :, LANES:] = qr_ref[:, hs]
        _flash_init(m_ref, l_ref, acc_ref, g)

    def tile(j, causal):
        rows = pl.ds(pl.multiple_of(j * tq, tq), tq)
        kr = kr_ref[rows, :]
        for g in range(heads):
            hs = slice(g * LANES, (g + 1) * LANES)
            k = jnp.concatenate([kn_ref[rows, hs], kr], axis=1)
            st = _dot_nt(k, q_ref[g])
            if causal:
                kidx = lax.broadcasted_iota(jnp.int32, st.shape, 0)
                qidx = lax.broadcasted_iota(jnp.int32, st.shape, 1)
                st = jnp.where(kidx <= qidx, st, NEG)
            _flash_step_t(st, vt_ref[hs, rows], m_ref, l_ref, acc_ref, g)

    def body(j, carry):
        tile(j, False)
        return carry

    lax.fori_loop(0, i, body, 0)
    tile(i, True)
    for g in range(heads):
        o_ref[:, g * LANES:(g + 1) * LANES] = _flash_out(l_ref, acc_ref, g).astype(o_ref.dtype)


def mla_attention(qn, qr, kn, kr, vt, tq=512, heads=2):
    S, W = qn.shape
    tq = min(tq, S)
    gw = heads * LANES
    qspec = pl.BlockSpec((tq, gw), lambda h, i: (i, h))
    return pl.pallas_call(
        functools.partial(_mla_attn_kernel, tq=tq, heads=heads),
        grid=(W // gw, S // tq),
        in_specs=[qspec, qspec,
                  pl.BlockSpec((S, gw), lambda h, i: (0, h)),
                  pl.BlockSpec((S, LANES), lambda h, i: (0, 0)),
                  pl.BlockSpec((gw, S), lambda h, i: (h, 0))],
        out_specs=qspec,
        out_shape=jax.ShapeDtypeStruct((S, W), BF16),
        scratch_shapes=[pltpu.VMEM((heads, tq, 2 * LANES), BF16), pltpu.VMEM((heads, 1, tq), F32),
                        pltpu.VMEM((heads, 8, tq), F32), pltpu.VMEM((heads, LANES, tq), F32)],
        compiler_params=_params("parallel", "arbitrary"),
        name="mla_attention",
    )(qn, qr, kn, kr, vt)


def _proj_res_kernel(o_ref, w_ref, h_ref, gpost_ref, gnext_ref, hn_ref, xn_ref, mix_ref):
    o = o_ref[...]
    n = w_ref.shape[1]
    chunk = 4 * LANES
    for c in range(n // chunk):
        sl = slice(c * chunk, (c + 1) * chunk)
        mix_ref[:, sl] = _dot(o, w_ref[:, sl])
    mix = mix_ref[...]
    hn = h_ref[...] + mix * _rms_scale(mix) * gpost_ref[...]
    hn_ref[...] = hn
    xn_ref[...] = (hn * _rms_scale(hn) * gnext_ref[...]).astype(xn_ref.dtype)


def proj_res_norm(o, w, h, g_post, g_next, tm=256):
    S, K = o.shape
    D = w.shape[1]
    tm = min(tm, S)
    row = lambda n: pl.BlockSpec((tm, n), lambda i: (i, 0))
    gspec = pl.BlockSpec((1, D), lambda i: (0, 0))
    return pl.pallas_call(
        _proj_res_kernel,
        grid=(S // tm,),
        in_specs=[row(K), pl.BlockSpec((K, D), lambda i: (0, 0)), row(D), gspec, gspec],
        out_specs=[row(D), row(D)],
        out_shape=[jax.ShapeDtypeStruct((S, D), F32), jax.ShapeDtypeStruct((S, D), BF16)],
        scratch_shapes=[pltpu.VMEM((tm, D), F32)],
        compiler_params=_params("parallel"),
        name="proj_res_norm",
    )(o, w, h, g_post.reshape(1, D), g_next.reshape(1, D))


HALO = 16


def _gelu_tanh(x):
    return 0.5 * x * (1.0 + jnp.tanh(math.sqrt(2.0 / math.pi) * (x + 0.044715 * (x * x * x))))


def _ffn_kernel(xn_ref, halo_ref, wg_ref, wu_ref, cwg_ref, cwu_ref, cbg_ref, cbu_ref, wo_ref, h_ref,
                gpost_ref, gnext_ref, hn_ref, *rest, n_next):
    xn_next_refs = rest[:n_next]
    xcat_ref, acc_ref = rest[n_next:]
    i = pl.program_id(0)
    c = pl.program_id(1)
    tm = xn_ref.shape[0]

    @pl.when(c == 0)
    def _():
        halo = halo_ref[...]
        xcat_ref[0:HALO, :] = jnp.where(i == 0, jnp.zeros_like(halo), halo)
        xcat_ref[HALO:, :] = xn_ref[...]
        acc_ref[...] = jnp.zeros(acc_ref.shape, F32)

    xcat = xcat_ref[...]

    def conv(w_ref, cw_ref, cb_ref):
        y = _dot(xcat, w_ref[...])
        cw = cw_ref[...]
        out = (y[HALO - 2:HALO - 2 + tm] * cw[0:1] + y[HALO - 1:HALO - 1 + tm] * cw[1:2]
               + y[HALO:] * cw[2:3])
        return out + cb_ref[...]

    act = _gelu_tanh(conv(wg_ref, cwg_ref, cbg_ref)) * conv(wu_ref, cwu_ref, cbu_ref)
    acc_ref[...] += _dot(act.astype(BF16), wo_ref[...])

    @pl.when(c == pl.num_programs(1) - 1)
    def _():
        f = acc_ref[...]
        hn = h_ref[...] + f * _rms_scale(f) * gpost_ref[...]
        hn_ref[...] = hn
        if n_next:
            y = hn * _rms_scale(hn)
            for k in range(n_next):
                xn_next_refs[k][...] = (y * gnext_ref[k:k + 1, :]).astype(BF16)


def ffn(xn, w_in, conv_w, conv_b, w_out, h, g_post, g_next, tm=512, tf=512):
    S, D = xn.shape
    FF = w_out.shape[0]
    tm = min(tm, S)
    nf = FF // tf
    n_next = 0 if g_next is None else g_next.shape[0]
    if g_next is None:
        g_next = jnp.ones((1, D), F32)
    cb = conv_b.reshape(1, 2 * FF)
    hb = tm // HALO
    row = pl.BlockSpec((tm, D), lambda i, c: (i, 0))
    in_specs = [
        row,
        pl.BlockSpec((HALO, D), lambda i, c: (jnp.maximum(i * hb - 1, 0), 0)),
        pl.BlockSpec((D, tf), lambda i, c: (0, c)),
        pl.BlockSpec((D, tf), lambda i, c: (0, c + nf)),
        pl.BlockSpec((CONV_WIDTH, tf), lambda i, c: (0, c)),
        pl.BlockSpec((CONV_WIDTH, tf), lambda i, c: (0, c + nf)),
        pl.BlockSpec((1, tf), lambda i, c: (0, c)),
        pl.BlockSpec((1, tf), lambda i, c: (0, c + nf)),
        pl.BlockSpec((tf, D), lambda i, c: (c, 0)),
        row,
        pl.BlockSpec((1, D), lambda i, c: (0, 0)),
        pl.BlockSpec(g_next.shape, lambda i, c: (0, 0)),
    ]
    outs = pl.pallas_call(
        functools.partial(_ffn_kernel, n_next=n_next),
        grid=(S // tm, nf),
        in_specs=in_specs,
        out_specs=[row] * (1 + n_next),
        out_shape=[jax.ShapeDtypeStruct((S, D), F32)] + [jax.ShapeDtypeStruct((S, D), BF16)] * n_next,
        scratch_shapes=[pltpu.VMEM((HALO + tm, D), BF16), pltpu.VMEM((tm, D), F32)],
        compiler_params=_params("parallel", "arbitrary"),
        name="conv_glu_ffn",
    )(xn, xn, w_in, w_in, conv_w, conv_w, cb, cb, w_out, h, g_post.reshape(1, D), g_next)
    return outs[0], list(outs[1:])


def _matmul_kernel(x_ref, w_ref, o_ref, *mean_ref, scale):
    y = _dot(x_ref[...], w_ref[...])
    if scale != 1.0:
        y = y * scale
    o_ref[...] = y.astype(o_ref.dtype)
    if mean_ref:
        mean_ref[0][...] = jnp.mean(y, axis=0, keepdims=True)[None]


def matmul(x, w, scale=1.0, tm=512, tn=1024, with_mean=False):
    S, K = x.shape
    N = w.shape[1]
    tm = min(tm, S)
    out_specs = [pl.BlockSpec((tm, tn), lambda n, i: (i, n))]
    out_shape = [jax.ShapeDtypeStruct((S, N), BF16)]
    if with_mean:
        out_specs.append(pl.BlockSpec((1, 1, tn), lambda n, i: (i, 0, n)))
        out_shape.append(jax.ShapeDtypeStruct((S // tm, 1, N), F32))
    outs = pl.pallas_call(
        functools.partial(_matmul_kernel, scale=scale),
        grid=(N // tn, S // tm),
        in_specs=[pl.BlockSpec((tm, K), lambda n, i: (i, 0)), pl.BlockSpec((K, tn), lambda n, i: (0, n))],
        out_specs=out_specs,
        out_shape=out_shape,
        compiler_params=_params("parallel", "parallel"),
        name="matmul_mean" if with_mean else "matmul",
    )(x, w)
    return outs if with_mean else outs[0]


def _matmul_nt_kernel(a_ref, b_ref, o_ref):
    o_ref[...] = _dot_nt(a_ref[...], b_ref[...]).astype(o_ref.dtype)


def matmul_nt(a, b, tm=1024, tn=512):
    M, K = a.shape
    N = b.shape[0]
    tm, tn = min(tm, M), min(tn, N)
    return pl.pallas_call(
        _matmul_nt_kernel,
        grid=(M // tm, N // tn),
        in_specs=[pl.BlockSpec((tm, K), lambda m, n: (m, 0)), pl.BlockSpec((tn, K), lambda m, n: (n, 0))],
        out_specs=pl.BlockSpec((tm, tn), lambda m, n: (m, n)),
        out_shape=jax.ShapeDtypeStruct((M, N), BF16),
        compiler_params=_params("parallel", "parallel"),
        name="matmul_nt",
    )(a, b)


def _rel_bucket(dist):
    n = jnp.maximum(dist, 0)
    max_exact = REL_BUCKETS // 2
    nf = jnp.maximum(n, 1).astype(F32)
    large = max_exact + (jnp.log(nf / max_exact) / math.log(REL_MAX_DIST / max_exact)
                         * (REL_BUCKETS - max_exact)).astype(jnp.int32)
    large = jnp.minimum(large, REL_BUCKETS - 1)
    return jnp.where(n < max_exact, n, large)


def _moba_kernel(qmin_ref, kmax_ref, q_ref, k_ref, vt_ref, kmean_ref, posq_ref, posk_ref, tbl_ref,
                 o_ref, qaug_ref, m_ref, l_ref, acc_ref, *, heads, nb):
    hg = pl.program_id(0)
    i = pl.program_id(1)
    L = MOBA_BLOCK
    P = 2 * L
    nbp = min(LANES, -(-nb // 8) * 8)
    blk = lax.broadcasted_iota(jnp.int32, (nbp, L), 0)
    for g in range(heads):
        hs = slice(g * LANES, (g + 1) * LANES)
        q = q_ref[:, hs]
        gate = _dot_nt(kmean_ref[:, hs].astype(BF16), q)[:nbp]
        gate = jnp.where(blk < i, gate, -jnp.inf)
        keep = jnp.zeros((nbp, L), F32)
        for _ in range(MOBA_TOPK):
            mx = jnp.max(gate, axis=0, keepdims=True)
            first = jnp.min(jnp.where(gate == mx, blk, nbp), axis=0, keepdims=True)
            hit = blk == first
            keep = jnp.where(hit, 1.0, keep)
            gate = jnp.where(hit, -jnp.inf, gate)
        keep = jnp.where(blk < i, keep, jnp.where(blk == i, 1.0, 0.0))
        pen = jnp.where(keep > 0.0, 0.0, NEG)
        if nbp < LANES:
            pen = jnp.concatenate([pen, jnp.zeros((LANES - nbp, L), F32)], axis=0)
        qaug_ref[g, :, 0:LANES] = q
        qaug_ref[g, :, LANES:] = pen.T.astype(BF16)
        _flash_init(m_ref, l_ref, acc_ref, g)

    lane = lax.broadcasted_iota(jnp.int32, (P, LANES), 1)
    half = jnp.where(lax.broadcasted_iota(jnp.int32, (P, LANES), 0) >= L, 1, 0)
    posq = posq_ref[0]

    def scores(p):
        rows = pl.ds(pl.multiple_of(p * P, P), P)
        onehot = jnp.where(lane == 2 * p + half, 1.0, 0.0).astype(BF16)
        out = []
        for g in range(heads):
            kaug = jnp.concatenate([k_ref[rows, g * LANES:(g + 1) * LANES], onehot], axis=1)
            out.append(_dot_nt(kaug, qaug_ref[g]))
        return out

    def biased(ss, p):
        rows = pl.ds(pl.multiple_of(p * P, P), P)
        bucket = _rel_bucket(posq - posk_ref[rows, :])
        out = []
        for g in range(heads):
            tb = jnp.broadcast_to(tbl_ref[pl.ds(hg * heads + g, 1), :], (P, LANES))
            b = [jnp.take_along_axis(tb, bucket[:, t * LANES:(t + 1) * LANES], axis=1)
                 for t in range(L // LANES)]
            out.append(ss[g] + jnp.concatenate(b, axis=1))
        return out

    def update(ss, p):
        cols = pl.ds(pl.multiple_of(p * P, P), P)
        for g in range(heads):
            _flash_step_t(ss[g], vt_ref[g * LANES:(g + 1) * LANES, cols], m_ref, l_ref, acc_ref, g)

    def body(p, carry):
        ss = scores(p)
        near = qmin_ref[i] - jnp.maximum(kmax_ref[2 * p], kmax_ref[2 * p + 1]) < FAR_DIST
        ss = lax.cond(near, lambda: biased(ss, p), lambda: ss)
        update(ss, p)
        return carry

    lax.fori_loop(0, i // 2, body, 0)
    p = i // 2
    ss = biased(scores(p), p)
    kidx = p * P + lax.broadcasted_iota(jnp.int32, (P, L), 0)
    qidx = i * L + lax.broadcasted_iota(jnp.int32, (P, L), 1)
    update([jnp.where(kidx <= qidx, s, NEG) for s in ss], p)
    for g in range(heads):
        o_ref[:, g * LANES:(g + 1) * LANES] = _flash_out(l_ref, acc_ref, g).astype(o_ref.dtype)


def moba_attention(q, k, vt, kmean_p, pos_blk, pos_col, tbl, qmin, kmax, heads=4):
    S, W = q.shape
    L = MOBA_BLOCK
    NB = S // L
    assert NB % 2 == 0
    gw = heads * LANES
    qspec = pl.BlockSpec((L, gw), lambda h, i, *_: (i, h))
    grid_spec = pltpu.PrefetchScalarGridSpec(
        num_scalar_prefetch=2,
        grid=(W // gw, NB),
        in_specs=[
            qspec,
            pl.BlockSpec((S, gw), lambda h, i, *_: (0, h)),
            pl.BlockSpec((gw, S), lambda h, i, *_: (h, 0)),
            pl.BlockSpec((kmean_p.shape[0], gw), lambda h, i, *_: (0, h)),
            pl.BlockSpec((1, 1, L), lambda h, i, *_: (i, 0, 0)),
            pl.BlockSpec((S, 1), lambda h, i, *_: (0, 0)),
            pl.BlockSpec(tbl.shape, lambda h, i, *_: (0, 0)),
        ],
        out_specs=qspec,
        scratch_shapes=[pltpu.VMEM((heads, L, 2 * LANES), BF16), pltpu.VMEM((heads, 1, L), F32),
                        pltpu.VMEM((heads, 8, L), F32), pltpu.VMEM((heads, LANES, L), F32)],
    )
    return pl.pallas_call(
        functools.partial(_moba_kernel, heads=heads, nb=NB),
        grid_spec=grid_spec,
        out_shape=jax.ShapeDtypeStruct((S, W), BF16),
        compiler_params=_params("parallel", "arbitrary"),
        name="moba_attention",
    )(qmin, kmax, q, k, vt, kmean_p, pos_blk.reshape(NB, 1, L), pos_col, tbl)


def _rope_lanes(w):
    half = QK_ROPE // 2
    z = jnp.zeros(w.shape[:-1] + (half,), w.dtype)
    return jnp.concatenate([w[..., :half], z, w[..., half:], z], axis=-1)


def _prep_mla(w_in, w_q_up, w_kv_up):
    w1 = jnp.concatenate([w_in[:, :Q_LORA + KV_LORA], _rope_lanes(w_in[:, Q_LORA + KV_LORA:])], axis=1)
    wq = w_q_up.reshape(Q_LORA, MLA_HEADS, QK_NOPE + QK_ROPE)
    wqn = wq[:, :, :QK_NOPE].reshape(Q_LORA, MLA_HEADS * QK_NOPE)
    wqr = _rope_lanes(wq[:, :, QK_NOPE:]).reshape(Q_LORA, MLA_HEADS * LANES)
    wkv = w_kv_up.reshape(KV_LORA, MLA_HEADS, QK_NOPE + V_HEAD)
    wkn = wkv[:, :, :QK_NOPE].reshape(KV_LORA, -1)
    wvt = wkv[:, :, QK_NOPE:].reshape(KV_LORA, -1).T
    return [w.astype(BF16) for w in (w1, wqn, wqr, wkn, wvt)]


def kernel(x, positions, norm_gains, a_w_in, a_q_norm, a_w_q_up, a_kv_norm, a_w_kv_up, a_w_o, b_kv_norm, b_w_kv,
           b_w_q, b_w_o, rel_bias, ffn_w_in, ffn_conv_w, ffn_conv_b, ffn_w_out):
    B, S, D = x.shape
    depth = norm_gains.shape[0]
    n_a = a_w_in.shape[0]
    L = MOBA_BLOCK
    NB = S // L
    HW = MOBA_HEADS * MOBA_HEAD
    outs = []
    for b in range(B):
        pos = positions[b]
        pos_col = pos.reshape(S, 1)
        pos_blk = pos.reshape(NB, L)
        qmin = jnp.min(pos_blk, axis=1)
        kmax = jnp.max(pos_blk, axis=1)
        cos_t, sin_t = rope_tables(pos_col)
        tbl = (rel_bias - rel_bias[REL_BUCKETS - 1:REL_BUCKETS, :]).T * LOG2E
        tbl = jnp.pad(tbl, ((0, 0), (0, LANES - REL_BUCKETS)))

        h = x[b]
        xn = norm_cast(h, norm_gains[0, 0])
        k = vt = kmean_p = None
        for layer in range(depth):
            g = norm_gains[layer]
            if layer < n_a:
                w1, wqn, wqr, wkn, wvt = _prep_mla(a_w_in[layer], a_w_q_up[layer], a_w_kv_up[layer])
                q_scale = (QK_NOPE + QK_ROPE) ** -0.5 * LOG2E
                qn, qr, kn, kr, vt_a = mla_proj(xn, w1, a_q_norm[layer].reshape(1, -1),
                                                a_kv_norm[layer].reshape(1, -1), wqn, wqr, wkn, wvt,
                                                cos_t, sin_t, q_scale)
                o = mla_attention(qn, qr, kn, kr, vt_a)
                w_o = a_w_o[layer]
            else:
                j = layer - n_a
                q = matmul(xn, b_w_q[j].astype(BF16), scale=MOBA_HEAD ** -0.5 * LOG2E)
                o = moba_attention(q, k, vt, kmean_p, pos_blk, pos_col, tbl, qmin, kmax)
                w_o = b_w_o[j]
            h, xn = proj_res_norm(o, w_o.astype(BF16), h, g[1], g[2])
            if layer + 1 == depth:
                g_next = None
            elif layer + 1 == n_a:
                g_next = jnp.stack([norm_gains[layer + 1, 0], b_kv_norm])
            else:
                g_next = norm_gains[layer + 1, 0].reshape(1, D)
            h, nxt = ffn(xn, ffn_w_in[layer].astype(BF16), ffn_conv_w[layer], ffn_conv_b[layer],
                         ffn_w_out[layer].astype(BF16), h, g[3], g_next)
            if nxt:
                xn = nxt[0]
            if layer + 1 == n_a:
                k, kmean = matmul(nxt[1], b_w_kv[:, :HW].astype(BF16), tm=L, with_mean=True)
                vt = matmul_nt(b_w_kv[:, HW:].T.astype(BF16), nxt[1])
                kmean_p = jnp.pad(kmean.reshape(NB, HW), ((0, LANES - NB), (0, 0)))
        outs.append(h)
    return jnp.stack(outs)
```

```python
import functools
import math

import jax
import jax.numpy as jnp
from jax import lax
from jax.experimental import pallas as pl
from jax.experimental.pallas import tpu as pltpu

F32 = jnp.float32
BF16 = jnp.bfloat16

MLA_HEADS = 16
Q_LORA = 512
KV_LORA = 512
QK_NOPE = 128
QK_ROPE = 64
V_HEAD = 128
ROPE_THETA = 10000.0
MOBA_HEADS = 16
MOBA_HEAD = 128
MOBA_BLOCK = 256
MOBA_TOPK = 3
REL_BUCKETS = 32
REL_MAX_DIST = 128
CONV_WIDTH = 3
EPS = 1e-6
NEG = -1e30
LOG2E = 1.4426950408889634

LANES = 128
FAR_DIST = REL_MAX_DIST
VMEM_LIMIT = 56 * 1024 * 1024


def _params(*sem):
    return pltpu.CompilerParams(dimension_semantics=sem, vmem_limit_bytes=VMEM_LIMIT)


def _rms_scale(x):
    return lax.rsqrt(jnp.mean(x * x, axis=-1, keepdims=True) + EPS)


def _dot(a, b):
    return jnp.dot(a, b, preferred_element_type=F32)


def _dot_nt(a, b):
    return lax.dot_general(a, b, (((1,), (1,)), ((), ())), preferred_element_type=F32)


def _norm_kernel(x_ref, g_ref, o_ref):
    x = x_ref[...]
    o_ref[...] = (x * _rms_scale(x) * g_ref[...]).astype(o_ref.dtype)


def norm_cast(x, g, tm=512):
    S, D = x.shape
    tm = min(tm, S)
    return pl.pallas_call(
        _norm_kernel,
        grid=(S // tm,),
        in_specs=[pl.BlockSpec((tm, D), lambda i: (i, 0)), pl.BlockSpec((1, D), lambda i: (0, 0))],
        out_specs=pl.BlockSpec((tm, D), lambda i: (i, 0)),
        out_shape=jax.ShapeDtypeStruct((S, D), BF16),
        compiler_params=_params("parallel"),
        name="norm_cast",
    )(x, g.reshape(1, D))


def _rope_table_kernel(pos_ref, inv_ref, sgn_ref, cos_ref, sin_ref):
    ang = pos_ref[...].astype(F32) * inv_ref[...]
    cos_ref[...] = jnp.cos(ang)
    sin_ref[...] = jnp.sin(ang) * sgn_ref[...]


def rope_tables(pos_col, tm=1024):
    S = pos_col.shape[0]
    tm = min(tm, S)
    half = QK_ROPE // 2
    inv = ROPE_THETA ** (-jnp.arange(half, dtype=F32) / half)
    z = jnp.zeros((half,), F32)
    inv_pat = jnp.concatenate([inv, z, inv, z]).reshape(1, LANES)
    o = jnp.ones((2 * half,), F32)
    sgn = jnp.concatenate([-o, o]).reshape(1, LANES)
    return pl.pallas_call(
        _rope_table_kernel,
        grid=(S // tm,),
        in_specs=[pl.BlockSpec((tm, 1), lambda i: (i, 0)),
                  pl.BlockSpec((1, LANES), lambda i: (0, 0)),
                  pl.BlockSpec((1, LANES), lambda i: (0, 0))],
        out_specs=[pl.BlockSpec((tm, LANES), lambda i: (i, 0))] * 2,
        out_shape=[jax.ShapeDtypeStruct((S, LANES), F32)] * 2,
        compiler_params=_params("parallel"),
        name="rope_tables",
    )(pos_col, inv_pat, sgn)


def _mla_proj_kernel(xn_ref, w1_ref, gq_ref, gkv_ref, wqn_ref, wqr_ref, wkn_ref, wvt_ref, cos_ref, sin_ref,
                     qn_ref, qr_ref, kn_ref, kr_ref, vt_ref, *, q_scale):
    xn = xn_ref[...]
    cos = cos_ref[...]
    sin = sin_ref[...]

    def rope(x):
        return x * cos + pltpu.roll(x, LANES // 2, axis=1) * sin

    cq = _dot(xn, w1_ref[:, 0:Q_LORA])
    ckv = _dot(xn, w1_ref[:, Q_LORA:Q_LORA + KV_LORA])
    kr = _dot(xn, w1_ref[:, Q_LORA + KV_LORA:])
    kr_ref[...] = rope(kr).astype(BF16)
    cqn = (cq * _rms_scale(cq) * gq_ref[...]).astype(BF16)
    ckvn = (ckv * _rms_scale(ckv) * gkv_ref[...]).astype(BF16)
    width = MLA_HEADS * QK_NOPE
    chunk = 4 * LANES
    for c in range(width // chunk):
        sl = slice(c * chunk, (c + 1) * chunk)
        qn_ref[:, sl] = (_dot(cqn, wqn_ref[:, sl]) * q_scale).astype(BF16)
        qr = _dot(cqn, wqr_ref[:, sl])
        for hh in range(chunk // LANES):
            x = qr[:, hh * LANES:(hh + 1) * LANES]
            lo = c * chunk + hh * LANES
            qr_ref[:, lo:lo + LANES] = (rope(x) * q_scale).astype(BF16)
        kn_ref[:, sl] = _dot(ckvn, wkn_ref[:, sl]).astype(BF16)
        vt_ref[sl, :] = _dot_nt(wvt_ref[sl, :], ckvn).astype(BF16)


def mla_proj(xn, w1, gq, gkv, wqn, wqr, wkn, wvt, cos_t, sin_t, q_scale, tm=256):
    S, D = xn.shape
    tm = min(tm, S)
    W = MLA_HEADS * QK_NOPE
    full = lambda a: pl.BlockSpec(a.shape, lambda i: (0, 0))
    row = lambda n: pl.BlockSpec((tm, n), lambda i: (i, 0))
    return pl.pallas_call(
        functools.partial(_mla_proj_kernel, q_scale=q_scale),
        grid=(S // tm,),
        in_specs=[row(D), full(w1), full(gq), full(gkv), full(wqn), full(wqr), full(wkn), full(wvt),
                  row(LANES), row(LANES)],
        out_specs=[row(W), row(W), row(W), row(LANES), pl.BlockSpec((W, tm), lambda i: (0, i))],
        out_shape=[jax.ShapeDtypeStruct((S, W), BF16), jax.ShapeDtypeStruct((S, W), BF16),
                   jax.ShapeDtypeStruct((S, W), BF16), jax.ShapeDtypeStruct((S, LANES), BF16),
                   jax.ShapeDtypeStruct((W, S), BF16)],
        compiler_params=_params("parallel"),
        name="mla_proj",
    )(xn, w1, gq, gkv, wqn, wqr, wkn, wvt, cos_t, sin_t)


def _flash_init(m_ref, l_ref, acc_ref, g):
    m_ref[g] = jnp.full(m_ref.shape[1:], NEG, F32)
    l_ref[g] = jnp.zeros(l_ref.shape[1:], F32)
    acc_ref[g] = jnp.zeros(acc_ref.shape[1:], F32)


def _flash_softmax(st, m_ref, l_ref, g, off=None):
    tk, tq = st.shape
    m_prev = m_ref[g]
    m_new = jnp.maximum(m_prev, jnp.max(st, axis=0, keepdims=True))
    alpha = jnp.exp2(m_prev - m_new)
    p = jnp.exp2(st - (m_new if off is None else m_new + off))
    l_ref[g] = alpha * l_ref[g] + jnp.sum(p.reshape(tk // 8, 8, tq), axis=0)
    m_ref[g] = m_new
    return p.astype(BF16), alpha


def _flash_accumulate(vt, p, alpha, acc_ref, g):
    acc_ref[g] = alpha * acc_ref[g] + _dot(vt, p)


def _flash_out(l_ref, acc_ref, g):
    l = jnp.sum(l_ref[g], axis=0, keepdims=True)
    return (acc_ref[g] / l).T


def _flash_pipeline(n, heads, score, vt_tile, fix_last, off, s_ref, p_ref, a_ref, m_ref, l_ref, acc_ref):
    for g in range(heads):
        _flash_init(m_ref, l_ref, acc_ref, g)
        p_ref[g, 1] = jnp.zeros(p_ref.shape[2:], BF16)
        a_ref[g, 1] = jnp.ones(a_ref.shape[2:], F32)
        s_ref[g, 0] = score(0, g)

    def tick(t, a, b):
        for g in range(heads):
            s_ref[g, b] = score(t + 1, g)
        tp = jnp.maximum(t - 1, 0)
        for g in range(heads):
            _flash_accumulate(vt_tile(tp, g), p_ref[g, b], a_ref[g, b], acc_ref, g)
        o = None if off is None else off(t)
        for g in range(heads):
            p, alpha = _flash_softmax(s_ref[g, a], m_ref, l_ref, g, o)
            p_ref[g, a] = p
            a_ref[g, a] = alpha

    def body(k, carry):
        tick(2 * k, 0, 1)
        tick(2 * k + 1, 1, 0)
        return carry

    lax.fori_loop(0, n // 2, body, 0)

    @pl.when(n % 2 == 1)
    def _():
        tick(n - 1, 0, 1)

    def last(e):
        tp = jnp.maximum(n - 1, 0)
        for g in range(heads):
            _flash_accumulate(vt_tile(tp, g), p_ref[g, 1 - e], a_ref[g, 1 - e], acc_ref, g)
        ss = fix_last([s_ref[g, e] for g in range(heads)])
        for g in range(heads):
            p, alpha = _flash_softmax(ss[g], m_ref, l_ref, g)
            _flash_accumulate(vt_tile(n, g), p, alpha, acc_ref, g)

    @pl.when(n % 2 == 0)
    def _():
        last(0)

    @pl.when(n % 2 == 1)
    def _():
        last(1)


def _flash_scratch(heads, tk, tq):
    return [pltpu.VMEM((heads, 2, tk, tq), F32), pltpu.VMEM((heads, 2, tk, tq), BF16),
            pltpu.VMEM((heads, 2, 1, tq), F32), pltpu.VMEM((heads, 1, tq), F32),
            pltpu.VMEM((heads, 8, tq), F32), pltpu.VMEM((heads, LANES, tq), F32)]


def _mla_attn_kernel(qn_ref, qr_ref, kn_ref, kr_ref, vt_ref, o_ref, q_ref, *flash_refs, tq, heads):
    i = pl.program_id(1)
    for g in range(heads):
        hs = slice(g * LANES, (g + 1) * LANES)
        q_ref[g, :, 0:LANES] = qn_ref[:, hs]
        q_ref[g, :, LANES:] = qr_ref[:, hs]

    def rows(t):
        return pl.ds(pl.multiple_of(t * tq, tq), tq)

    def score(t, g):
        k = jnp.concatenate([kn_ref[rows(t), g * LANES:(g + 1) * LANES], kr_ref[rows(t), :]], axis=1)
        return _dot_nt(k, q_ref[g])

    def vt_tile(t, g):
        return vt_ref[g * LANES:(g + 1) * LANES, rows(t)]

    def causal(ss):
        kidx = lax.broadcasted_iota(jnp.int32, (tq, tq), 0)
        qidx = lax.broadcasted_iota(jnp.int32, (tq, tq), 1)
        return [jnp.where(kidx <= qidx, s, NEG) for s in ss]

    _flash_pipeline(i, heads, score, vt_tile, causal, None, *flash_refs)
    l_ref, acc_ref = flash_refs[-2:]
    for g in range(heads):
        o_ref[:, g * LANES:(g + 1) * LANES] = _flash_out(l_ref, acc_ref, g).astype(o_ref.dtype)


def mla_attention(qn, qr, kn, kr, vt, tq=512, heads=2):
    S, W = qn.shape
    tq = min(tq, S)
    gw = heads * LANES
    qspec = pl.BlockSpec((tq, gw), lambda h, i: (i, h))
    return pl.pallas_call(
        functools.partial(_mla_attn_kernel, tq=tq, heads=heads),
        grid=(W // gw, S // tq),
        in_specs=[qspec, qspec,
                  pl.BlockSpec((S, gw), lambda h, i: (0, h)),
                  pl.BlockSpec((S, LANES), lambda h, i: (0, 0)),
                  pl.BlockSpec((gw, S), lambda h, i: (h, 0))],
        out_specs=qspec,
        out_shape=jax.ShapeDtypeStruct((S, W), BF16),
        scratch_shapes=[pltpu.VMEM((heads, tq, 2 * LANES), BF16)] + _flash_scratch(heads, tq, tq),
        compiler_params=_params("parallel", "arbitrary"),
        name="mla_attention",
    )(qn, qr, kn, kr, vt)


def _proj_res_kernel(o_ref, w_ref, h_ref, gpost_ref, gnext_ref, hn_ref, xn_ref, mix_ref):
    o = o_ref[...]
    n = w_ref.shape[1]
    chunk = 4 * LANES
    for c in range(n // chunk):
        sl = slice(c * chunk, (c + 1) * chunk)
        mix_ref[:, sl] = _dot(o, w_ref[:, sl])
    mix = mix_ref[...]
    hn = h_ref[...] + mix * _rms_scale(mix) * gpost_ref[...]
    hn_ref[...] = hn
    xn_ref[...] = (hn * _rms_scale(hn) * gnext_ref[...]).astype(xn_ref.dtype)


def proj_res_norm(o, w, h, g_post, g_next, tm=256):
    S, K = o.shape
    D = w.shape[1]
    tm = min(tm, S)
    row = lambda n: pl.BlockSpec((tm, n), lambda i: (i, 0))
    gspec = pl.BlockSpec((1, D), lambda i: (0, 0))
    return pl.pallas_call(
        _proj_res_kernel,
        grid=(S // tm,),
        in_specs=[row(K), pl.BlockSpec((K, D), lambda i: (0, 0)), row(D), gspec, gspec],
        out_specs=[row(D), row(D)],
        out_shape=[jax.ShapeDtypeStruct((S, D), F32), jax.ShapeDtypeStruct((S, D), BF16)],
        scratch_shapes=[pltpu.VMEM((tm, D), F32)],
        compiler_params=_params("parallel"),
        name="proj_res_norm",
    )(o, w, h, g_post.reshape(1, D), g_next.reshape(1, D))


HALO = 16


def _gelu_tanh(x):
    return 0.5 * x * (1.0 + jnp.tanh(math.sqrt(2.0 / math.pi) * (x + 0.044715 * (x * x * x))))


def _ffn_kernel(xn_ref, halo_ref, wg_ref, wu_ref, cwg_ref, cwu_ref, cbg_ref, cbu_ref, wo_ref, h_ref,
                gpost_ref, gnext_ref, hn_ref, *rest, n_next):
    xn_next_refs = rest[:n_next]
    xcat_ref, acc_ref = rest[n_next:]
    i = pl.program_id(0)
    c = pl.program_id(1)
    tm = xn_ref.shape[0]

    @pl.when(c == 0)
    def _():
        halo = halo_ref[...]
        xcat_ref[0:HALO, :] = jnp.where(i == 0, jnp.zeros_like(halo), halo)
        xcat_ref[HALO:, :] = xn_ref[...]
        acc_ref[...] = jnp.zeros(acc_ref.shape, F32)

    xcat = xcat_ref[...]

    def conv(w_ref, cw_ref, cb_ref):
        y = _dot(xcat, w_ref[...])
        cw = cw_ref[...]
        out = (y[HALO - 2:HALO - 2 + tm] * cw[0:1] + y[HALO - 1:HALO - 1 + tm] * cw[1:2]
               + y[HALO:] * cw[2:3])
        return out + cb_ref[...]

    act = _gelu_tanh(conv(wg_ref, cwg_ref, cbg_ref)) * conv(wu_ref, cwu_ref, cbu_ref)
    acc_ref[...] += _dot(act.astype(BF16), wo_ref[...])

    @pl.when(c == pl.num_programs(1) - 1)
    def _():
        f = acc_ref[...]
        hn = h_ref[...] + f * _rms_scale(f) * gpost_ref[...]
        hn_ref[...] = hn
        if n_next:
            y = hn * _rms_scale(hn)
            for k in range(n_next):
                xn_next_refs[k][...] = (y * gnext_ref[k:k + 1, :]).astype(BF16)


def ffn(xn, w_in, conv_w, conv_b, w_out, h, g_post, g_next, tm=512, tf=512):
    S, D = xn.shape
    FF = w_out.shape[0]
    tm = min(tm, S)
    nf = FF // tf
    n_next = 0 if g_next is None else g_next.shape[0]
    if g_next is None:
        g_next = jnp.ones((1, D), F32)
    cb = conv_b.reshape(1, 2 * FF)
    hb = tm // HALO
    row = pl.BlockSpec((tm, D), lambda i, c: (i, 0))
    in_specs = [
        row,
        pl.BlockSpec((HALO, D), lambda i, c: (jnp.maximum(i * hb - 1, 0), 0)),
        pl.BlockSpec((D, tf), lambda i, c: (0, c)),
        pl.BlockSpec((D, tf), lambda i, c: (0, c + nf)),
        pl.BlockSpec((CONV_WIDTH, tf), lambda i, c: (0, c)),
        pl.BlockSpec((CONV_WIDTH, tf), lambda i, c: (0, c + nf)),
        pl.BlockSpec((1, tf), lambda i, c: (0, c)),
        pl.BlockSpec((1, tf), lambda i, c: (0, c + nf)),
        pl.BlockSpec((tf, D), lambda i, c: (c, 0)),
        row,
        pl.BlockSpec((1, D), lambda i, c: (0, 0)),
        pl.BlockSpec(g_next.shape, lambda i, c: (0, 0)),
    ]
    outs = pl.pallas_call(
        functools.partial(_ffn_kernel, n_next=n_next),
        grid=(S // tm, nf),
        in_specs=in_specs,
        out_specs=[row] * (1 + n_next),
        out_shape=[jax.ShapeDtypeStruct((S, D), F32)] + [jax.ShapeDtypeStruct((S, D), BF16)] * n_next,
        scratch_shapes=[pltpu.VMEM((HALO + tm, D), BF16), pltpu.VMEM((tm, D), F32)],
        compiler_params=_params("parallel", "arbitrary"),
        name="conv_glu_ffn",
    )(xn, xn, w_in, w_in, conv_w, conv_w, cb, cb, w_out, h, g_post.reshape(1, D), g_next)
    return outs[0], list(outs[1:])


def _matmul_kernel(x_ref, w_ref, o_ref, *mean_ref, scale):
    y = _dot(x_ref[...], w_ref[...])
    if scale != 1.0:
        y = y * scale
    o_ref[...] = y.astype(o_ref.dtype)
    if mean_ref:
        mean_ref[0][...] = jnp.mean(y, axis=0, keepdims=True)[None]


def matmul(x, w, scale=1.0, tm=512, tn=1024, with_mean=False):
    S, K = x.shape
    N = w.shape[1]
    tm = min(tm, S)
    out_specs = [pl.BlockSpec((tm, tn), lambda n, i: (i, n))]
    out_shape = [jax.ShapeDtypeStruct((S, N), BF16)]
    if with_mean:
        out_specs.append(pl.BlockSpec((1, 1, tn), lambda n, i: (i, 0, n)))
        out_shape.append(jax.ShapeDtypeStruct((S // tm, 1, N), F32))
    outs = pl.pallas_call(
        functools.partial(_matmul_kernel, scale=scale),
        grid=(N // tn, S // tm),
        in_specs=[pl.BlockSpec((tm, K), lambda n, i: (i, 0)), pl.BlockSpec((K, tn), lambda n, i: (0, n))],
        out_specs=out_specs,
        out_shape=out_shape,
        compiler_params=_params("parallel", "parallel"),
        name="matmul_mean" if with_mean else "matmul",
    )(x, w)
    return outs if with_mean else outs[0]


def _matmul_nt_kernel(a_ref, b_ref, o_ref):
    o_ref[...] = _dot_nt(a_ref[...], b_ref[...]).astype(o_ref.dtype)


def matmul_nt(a, b, tm=1024, tn=512):
    M, K = a.shape
    N = b.shape[0]
    tm, tn = min(tm, M), min(tn, N)
    return pl.pallas_call(
        _matmul_nt_kernel,
        grid=(M // tm, N // tn),
        in_specs=[pl.BlockSpec((tm, K), lambda m, n: (m, 0)), pl.BlockSpec((tn, K), lambda m, n: (n, 0))],
        out_specs=pl.BlockSpec((tm, tn), lambda m, n: (m, n)),
        out_shape=jax.ShapeDtypeStruct((M, N), BF16),
        compiler_params=_params("parallel", "parallel"),
        name="matmul_nt",
    )(a, b)


def _rel_bucket(dist):
    n = jnp.maximum(dist, 0)
    max_exact = REL_BUCKETS // 2
    nf = jnp.maximum(n, 1).astype(F32)
    large = max_exact + (jnp.log(nf / max_exact) / math.log(REL_MAX_DIST / max_exact)
                         * (REL_BUCKETS - max_exact)).astype(jnp.int32)
    large = jnp.minimum(large, REL_BUCKETS - 1)
    return jnp.where(n < max_exact, n, large)


def _moba_kernel(qmin_ref, kmax_ref, q_ref, k_ref, vt_ref, kmean_ref, posq_ref, posk_ref, tbl_ref,
                 o_ref, qaug_ref, *flash_refs, heads, nb):
    hg = pl.program_id(0)
    i = pl.program_id(1)
    L = MOBA_BLOCK
    P = 2 * L
    n = i // 2
    nbp = min(LANES, -(-nb // 8) * 8)
    blk = lax.broadcasted_iota(jnp.int32, (nbp, L), 0)
    for g in range(heads):
        hs = slice(g * LANES, (g + 1) * LANES)
        q = q_ref[:, hs]
        gate = _dot_nt(kmean_ref[:, hs].astype(BF16), q)[:nbp]
        gate = jnp.where(blk < i, gate, -jnp.inf)
        keep = jnp.zeros((nbp, L), F32)
        for _ in range(MOBA_TOPK):
            mx = jnp.max(gate, axis=0, keepdims=True)
            first = jnp.min(jnp.where(gate == mx, blk, nbp), axis=0, keepdims=True)
            hit = blk == first
            keep = jnp.where(hit, 1.0, keep)
            gate = jnp.where(hit, -jnp.inf, gate)
        keep = jnp.where(blk < i, keep, jnp.where(blk == i, 1.0, 0.0))
        pen = jnp.where(keep > 0.0, 0.0, NEG)
        if nbp < LANES:
            pen = jnp.concatenate([pen, jnp.zeros((LANES - nbp, L), F32)], axis=0)
        qaug_ref[g, :, 0:LANES] = q
        qaug_ref[g, :, LANES:] = pen.T.astype(BF16)

    lane = lax.broadcasted_iota(jnp.int32, (P, LANES), 1)
    half = jnp.where(lax.broadcasted_iota(jnp.int32, (P, LANES), 0) >= L, 1, 0)
    posq = posq_ref[0]

    def rows(t):
        return pl.ds(pl.multiple_of(t * P, P), P)

    def score(t, g):
        onehot = jnp.where(lane == 2 * t + half, 1.0, 0.0).astype(BF16)
        kaug = jnp.concatenate([k_ref[rows(t), g * LANES:(g + 1) * LANES], onehot], axis=1)
        return _dot_nt(kaug, qaug_ref[g])

    def vt_tile(t, g):
        return vt_ref[g * LANES:(g + 1) * LANES, rows(t)]

    def biased(ss, t):
        posk = jnp.broadcast_to(posk_ref[t], (LANES, P)).T[:, 0:1]
        bucket = _rel_bucket(posq - posk)
        out = []
        for g in range(heads):
            tb = jnp.broadcast_to(tbl_ref[pl.ds(hg * heads + g, 1), :], (P, LANES))
            b = [jnp.take_along_axis(tb, bucket[:, c * LANES:(c + 1) * LANES], axis=1)
                 for c in range(L // LANES)]
            out.append(ss[g] + jnp.concatenate(b, axis=1))
        return out

    def near(t):
        return qmin_ref[i] - jnp.maximum(kmax_ref[2 * t], kmax_ref[2 * t + 1]) < FAR_DIST

    def own_pair(ss):
        kidx = n * P + lax.broadcasted_iota(jnp.int32, (P, L), 0)
        qidx = i * L + lax.broadcasted_iota(jnp.int32, (P, L), 1)
        return [jnp.where(kidx <= qidx, s, NEG) for s in biased(ss, n)]

    _flash_pipeline(n, heads, score, vt_tile, own_pair, lambda t: jnp.where(near(t), -NEG, 0.0), *flash_refs)
    m_ref, l_ref, acc_ref = flash_refs[-3:]

    def redo(t, carry):
        @pl.when(near(t))
        def _():
            ss = biased([score(t, g) for g in range(heads)], t)
            for g in range(heads):
                p, alpha = _flash_softmax(ss[g], m_ref, l_ref, g)
                _flash_accumulate(vt_tile(t, g), p, alpha, acc_ref, g)
        return carry

    lax.fori_loop(0, n, redo, 0)
    for g in range(heads):
        o_ref[:, g * LANES:(g + 1) * LANES] = _flash_out(l_ref, acc_ref, g).astype(o_ref.dtype)


def moba_attention(q, k, vt, kmean_p, pos_blk, tbl, qmin, kmax, heads=4):
    S, W = q.shape
    L = MOBA_BLOCK
    NB = S // L
    assert NB % 2 == 0
    gw = heads * LANES
    qspec = pl.BlockSpec((L, gw), lambda h, i, *_: (i, h))
    pos_pair = pos_blk.reshape(NB // 2, 1, 2 * L)
    grid_spec = pltpu.PrefetchScalarGridSpec(
        num_scalar_prefetch=2,
        grid=(W // gw, NB),
        in_specs=[
            qspec,
            pl.BlockSpec((S, gw), lambda h, i, *_: (0, h)),
            pl.BlockSpec((gw, S), lambda h, i, *_: (h, 0)),
            pl.BlockSpec((kmean_p.shape[0], gw), lambda h, i, *_: (0, h)),
            pl.BlockSpec((1, 1, L), lambda h, i, *_: (i, 0, 0)),
            pl.BlockSpec(pos_pair.shape, lambda h, i, *_: (0, 0, 0)),
            pl.BlockSpec(tbl.shape, lambda h, i, *_: (0, 0)),
        ],
        out_specs=qspec,
        scratch_shapes=[pltpu.VMEM((heads, L, 2 * LANES), BF16)] + _flash_scratch(heads, 2 * L, L),
    )
    return pl.pallas_call(
        functools.partial(_moba_kernel, heads=heads, nb=NB),
        grid_spec=grid_spec,
        out_shape=jax.ShapeDtypeStruct((S, W), BF16),
        compiler_params=_params("parallel", "arbitrary"),
        name="moba_attention",
    )(qmin, kmax, q, k, vt, kmean_p, pos_blk.reshape(NB, 1, L), pos_pair, tbl)


def _rope_lanes(w):
    half = QK_ROPE // 2
    z = jnp.zeros(w.shape[:-1] + (half,), w.dtype)
    return jnp.concatenate([w[..., :half], z, w[..., half:], z], axis=-1)


def _prep_mla(w_in, w_q_up, w_kv_up):
    w1 = jnp.concatenate([w_in[:, :Q_LORA + KV_LORA], _rope_lanes(w_in[:, Q_LORA + KV_LORA:])], axis=1)
    wq = w_q_up.reshape(Q_LORA, MLA_HEADS, QK_NOPE + QK_ROPE)
    wqn = wq[:, :, :QK_NOPE].reshape(Q_LORA, MLA_HEADS * QK_NOPE)
    wqr = _rope_lanes(wq[:, :, QK_NOPE:]).reshape(Q_LORA, MLA_HEADS * LANES)
    wkv = w_kv_up.reshape(KV_LORA, MLA_HEADS, QK_NOPE + V_HEAD)
    wkn = wkv[:, :, :QK_NOPE].reshape(KV_LORA, -1)
    wvt = wkv[:, :, QK_NOPE:].reshape(KV_LORA, -1).T
    return [w.astype(BF16) for w in (w1, wqn, wqr, wkn, wvt)]


def kernel(x, positions, norm_gains, a_w_in, a_q_norm, a_w_q_up, a_kv_norm, a_w_kv_up, a_w_o, b_kv_norm, b_w_kv,
           b_w_q, b_w_o, rel_bias, ffn_w_in, ffn_conv_w, ffn_conv_b, ffn_w_out):
    B, S, D = x.shape
    depth = norm_gains.shape[0]
    n_a = a_w_in.shape[0]
    L = MOBA_BLOCK
    NB = S // L
    HW = MOBA_HEADS * MOBA_HEAD
    outs = []
    for b in range(B):
        pos = positions[b]
        pos_col = pos.reshape(S, 1)
        pos_blk = pos.reshape(NB, L)
        qmin = jnp.min(pos_blk, axis=1)
        kmax = jnp.max(pos_blk, axis=1)
        cos_t, sin_t = rope_tables(pos_col)
        tbl = (rel_bias - rel_bias[REL_BUCKETS - 1:REL_BUCKETS, :]).T * LOG2E
        tbl = jnp.pad(tbl, ((0, 0), (0, LANES - REL_BUCKETS)))

        h = x[b]
        xn = norm_cast(h, norm_gains[0, 0])
        k = vt = kmean_p = None
        for layer in range(depth):
            g = norm_gains[layer]
            if layer < n_a:
                w1, wqn, wqr, wkn, wvt = _prep_mla(a_w_in[layer], a_w_q_up[layer], a_w_kv_up[layer])
                q_scale = (QK_NOPE + QK_ROPE) ** -0.5 * LOG2E
                qn, qr, kn, kr, vt_a = mla_proj(xn, w1, a_q_norm[layer].reshape(1, -1),
                                                a_kv_norm[layer].reshape(1, -1), wqn, wqr, wkn, wvt,
                                                cos_t, sin_t, q_scale)
                o = mla_attention(qn, qr, kn, kr, vt_a)
                w_o = a_w_o[layer]
            else:
                j = layer - n_a
                q = matmul(xn, b_w_q[j].astype(BF16), scale=MOBA_HEAD ** -0.5 * LOG2E)
                o = moba_attention(q, k, vt, kmean_p, pos_blk, tbl, qmin, kmax)
                w_o = b_w_o[j]
            h, xn = proj_res_norm(o, w_o.astype(BF16), h, g[1], g[2])
            if layer + 1 == depth:
                g_next = None
            elif layer + 1 == n_a:
                g_next = jnp.stack([norm_gains[layer + 1, 0], b_kv_norm])
            else:
                g_next = norm_gains[layer + 1, 0].reshape(1, D)
            h, nxt = ffn(xn, ffn_w_in[layer].astype(BF16), ffn_conv_w[layer], ffn_conv_b[layer],
                         ffn_w_out[layer].astype(BF16), h, g[3], g_next)
            if nxt:
                xn = nxt[0]
            if layer + 1 == n_a:
                k, kmean = matmul(nxt[1], b_w_kv[:, :HW].astype(BF16), tm=L, with_mean=True)
                vt = matmul_nt(b_w_kv[:, HW:].T.astype(BF16), nxt[1])
                kmean_p = jnp.pad(kmean.reshape(NB, HW), ((0, LANES - NB), (0, 0)))
        outs.append(h)
    return jnp.stack(outs)
```

```python
import functools
import math

import jax
import jax.numpy as jnp
from jax import lax
from jax.experimental import pallas as pl
from jax.experimental.pallas import tpu as pltpu

F32 = jnp.float32
BF16 = jnp.bfloat16

MLA_HEADS = 16
Q_LORA = 512
KV_LORA = 512
QK_NOPE = 128
QK_ROPE = 64
V_HEAD = 128
ROPE_THETA = 10000.0
MOBA_HEADS = 16
MOBA_HEAD = 128
MOBA_BLOCK = 256
MOBA_TOPK = 3
REL_BUCKETS = 32
REL_MAX_DIST = 128
CONV_WIDTH = 3
EPS = 1e-6
NEG = -1e30
LOG2E = 1.4426950408889634

LANES = 128
FAR_DIST = REL_MAX_DIST
VMEM_LIMIT = 56 * 1024 * 1024


def _params(*sem):
    return pltpu.CompilerParams(dimension_semantics=sem, vmem_limit_bytes=VMEM_LIMIT)


def _rms_scale(x):
    return lax.rsqrt(jnp.mean(x * x, axis=-1, keepdims=True) + EPS)


def _dot(a, b):
    return jnp.dot(a, b, preferred_element_type=F32)


def _dot_nt(a, b):
    return lax.dot_general(a, b, (((1,), (1,)), ((), ())), preferred_element_type=F32)


def _norm_kernel(x_ref, g_ref, o_ref):
    x = x_ref[...]
    o_ref[...] = (x * _rms_scale(x) * g_ref[...]).astype(o_ref.dtype)


def norm_cast(x, g, tm=512):
    S, D = x.shape
    tm = min(tm, S)
    return pl.pallas_call(
        _norm_kernel,
        grid=(S // tm,),
        in_specs=[pl.BlockSpec((tm, D), lambda i: (i, 0)), pl.BlockSpec((1, D), lambda i: (0, 0))],
        out_specs=pl.BlockSpec((tm, D), lambda i: (i, 0)),
        out_shape=jax.ShapeDtypeStruct((S, D), BF16),
        compiler_params=_params("parallel"),
        name="norm_cast",
    )(x, g.reshape(1, D))


def _rope_table_kernel(pos_ref, inv_ref, sgn_ref, cos_ref, sin_ref):
    ang = pos_ref[...].astype(F32) * inv_ref[...]
    cos_ref[...] = jnp.cos(ang)
    sin_ref[...] = jnp.sin(ang) * sgn_ref[...]


def rope_tables(pos_col, tm=1024):
    S = pos_col.shape[0]
    tm = min(tm, S)
    half = QK_ROPE // 2
    inv = ROPE_THETA ** (-jnp.arange(half, dtype=F32) / half)
    z = jnp.zeros((half,), F32)
    inv_pat = jnp.concatenate([inv, z, inv, z]).reshape(1, LANES)
    o = jnp.ones((2 * half,), F32)
    sgn = jnp.concatenate([-o, o]).reshape(1, LANES)
    return pl.pallas_call(
        _rope_table_kernel,
        grid=(S // tm,),
        in_specs=[pl.BlockSpec((tm, 1), lambda i: (i, 0)),
                  pl.BlockSpec((1, LANES), lambda i: (0, 0)),
                  pl.BlockSpec((1, LANES), lambda i: (0, 0))],
        out_specs=[pl.BlockSpec((tm, LANES), lambda i: (i, 0))] * 2,
        out_shape=[jax.ShapeDtypeStruct((S, LANES), F32)] * 2,
        compiler_params=_params("parallel"),
        name="rope_tables",
    )(pos_col, inv_pat, sgn)


def _mla_proj_kernel(xn_ref, w1_ref, gq_ref, gkv_ref, wqn_ref, wqr_ref, wkn_ref, wvt_ref, cos_ref, sin_ref,
                     qn_ref, qr_ref, kn_ref, kr_ref, vt_ref, *, q_scale):
    xn = xn_ref[...]
    cos = cos_ref[...]
    sin = sin_ref[...]

    def rope(x):
        return x * cos + pltpu.roll(x, LANES // 2, axis=1) * sin

    cq = _dot(xn, w1_ref[:, 0:Q_LORA])
    ckv = _dot(xn, w1_ref[:, Q_LORA:Q_LORA + KV_LORA])
    kr = _dot(xn, w1_ref[:, Q_LORA + KV_LORA:])
    kr_ref[...] = rope(kr).astype(BF16)
    cqn = (cq * _rms_scale(cq) * gq_ref[...]).astype(BF16)
    ckvn = (ckv * _rms_scale(ckv) * gkv_ref[...]).astype(BF16)
    width = MLA_HEADS * QK_NOPE
    chunk = 4 * LANES
    for c in range(width // chunk):
        sl = slice(c * chunk, (c + 1) * chunk)
        qn_ref[:, sl] = (_dot(cqn, wqn_ref[:, sl]) * q_scale).astype(BF16)
        qr = _dot(cqn, wqr_ref[:, sl])
        for hh in range(chunk // LANES):
            x = qr[:, hh * LANES:(hh + 1) * LANES]
            lo = c * chunk + hh * LANES
            qr_ref[:, lo:lo + LANES] = (rope(x) * q_scale).astype(BF16)
        kn_ref[:, sl] = _dot(ckvn, wkn_ref[:, sl]).astype(BF16)
        vt_ref[sl, :] = _dot_nt(wvt_ref[sl, :], ckvn).astype(BF16)


def mla_proj(xn, w1, gq, gkv, wqn, wqr, wkn, wvt, cos_t, sin_t, q_scale, tm=256):
    S, D = xn.shape
    tm = min(tm, S)
    W = MLA_HEADS * QK_NOPE
    full = lambda a: pl.BlockSpec(a.shape, lambda i: (0, 0))
    row = lambda n: pl.BlockSpec((tm, n), lambda i: (i, 0))
    return pl.pallas_call(
        functools.partial(_mla_proj_kernel, q_scale=q_scale),
        grid=(S // tm,),
        in_specs=[row(D), full(w1), full(gq), full(gkv), full(wqn), full(wqr), full(wkn), full(wvt),
                  row(LANES), row(LANES)],
        out_specs=[row(W), row(W), row(W), row(LANES), pl.BlockSpec((W, tm), lambda i: (0, i))],
        out_shape=[jax.ShapeDtypeStruct((S, W), BF16), jax.ShapeDtypeStruct((S, W), BF16),
                   jax.ShapeDtypeStruct((S, W), BF16), jax.ShapeDtypeStruct((S, LANES), BF16),
                   jax.ShapeDtypeStruct((W, S), BF16)],
        compiler_params=_params("parallel"),
        name="mla_proj",
    )(xn, w1, gq, gkv, wqn, wqr, wkn, wvt, cos_t, sin_t)


def _flash_init(m_ref, l_ref, acc_ref, g):
    m_ref[g] = jnp.full(m_ref.shape[1:], NEG, F32)
    l_ref[g] = jnp.zeros(l_ref.shape[1:], F32)
    acc_ref[g] = jnp.zeros(acc_ref.shape[1:], F32)


def _flash_softmax(st, m_ref, l_ref, g, off=None):
    tk, tq = st.shape
    m_prev = m_ref[g]
    m_new = jnp.maximum(m_prev, jnp.max(st, axis=0, keepdims=True))
    alpha = jnp.exp2(m_prev - m_new)
    p = jnp.exp2(st - (m_new if off is None else m_new + off))
    l_ref[g] = alpha * l_ref[g] + jnp.sum(p.reshape(tk // 8, 8, tq), axis=0)
    m_ref[g] = m_new
    return p.astype(BF16), alpha


def _flash_accumulate(vt, p, alpha, acc_ref, g):
    acc_ref[g] = alpha * acc_ref[g] + _dot(vt, p)


def _flash_out(l_ref, acc_ref, g):
    l = jnp.sum(l_ref[g], axis=0, keepdims=True)
    return (acc_ref[g] / l).T


def _flash_pipeline(n, heads, first_score, score, vt_tile, off, s_ref, p_ref, a_ref, m_ref, l_ref, acc_ref):
    for g in range(heads):
        _flash_init(m_ref, l_ref, acc_ref, g)
        p_ref[g, 1] = jnp.zeros(p_ref.shape[2:], BF16)
        a_ref[g, 1] = jnp.ones(a_ref.shape[2:], F32)
        s_ref[g, 0] = first_score(g)

    def tick(t, a, b):
        for g in range(heads):
            s_ref[g, b] = score(t + 1, g)
        tp = jnp.maximum(t - 1, 0)
        for g in range(heads):
            _flash_accumulate(vt_tile(tp, g), p_ref[g, b], a_ref[g, b], acc_ref, g)
        o = None if off is None else off(t)
        for g in range(heads):
            p, alpha = _flash_softmax(s_ref[g, a], m_ref, l_ref, g, o)
            p_ref[g, a] = p
            a_ref[g, a] = alpha

    def body(k, carry):
        tick(2 * k, 0, 1)
        tick(2 * k + 1, 1, 0)
        return carry

    lax.fori_loop(0, n // 2, body, 0)

    @pl.when(n % 2 == 1)
    def _():
        tick(n - 1, 0, 1)

    def last(e):
        tp = jnp.maximum(n - 1, 0)
        for g in range(heads):
            _flash_accumulate(vt_tile(tp, g), p_ref[g, 1 - e], a_ref[g, 1 - e], acc_ref, g)
        o = None if off is None else off(n)
        for g in range(heads):
            p, alpha = _flash_softmax(s_ref[g, e], m_ref, l_ref, g, o)
            _flash_accumulate(vt_tile(n, g), p, alpha, acc_ref, g)

    @pl.when(n % 2 == 0)
    def _():
        last(0)

    @pl.when(n % 2 == 1)
    def _():
        last(1)


def _flash_scratch(heads, tk, tq):
    return [pltpu.VMEM((heads, 2, tk, tq), F32), pltpu.VMEM((heads, 2, tk, tq), BF16),
            pltpu.VMEM((heads, 2, 1, tq), F32), pltpu.VMEM((heads, 1, tq), F32),
            pltpu.VMEM((heads, 8, tq), F32), pltpu.VMEM((heads, LANES, tq), F32)]


def _mla_attn_kernel(qn_ref, qr_ref, kn_ref, kr_ref, vt_ref, o_ref, q_ref, *flash_refs, tq, heads):
    i = pl.program_id(1)
    for g in range(heads):
        hs = slice(g * LANES, (g + 1) * LANES)
        q_ref[g, :, 0:LANES] = qn_ref[:, hs]
        q_ref[g, :, LANES:] = qr_ref[:, hs]

    def rows(t):
        return pl.ds(pl.multiple_of(jnp.where(t == 0, i, t - 1) * tq, tq), tq)

    def score(t, g):
        k = jnp.concatenate([kn_ref[rows(t), g * LANES:(g + 1) * LANES], kr_ref[rows(t), :]], axis=1)
        return _dot_nt(k, q_ref[g])

    def vt_tile(t, g):
        return vt_ref[g * LANES:(g + 1) * LANES, rows(t)]

    def diagonal(g):
        kidx = lax.broadcasted_iota(jnp.int32, (tq, tq), 0)
        qidx = lax.broadcasted_iota(jnp.int32, (tq, tq), 1)
        return jnp.where(kidx <= qidx, score(0, g), NEG)

    _flash_pipeline(i, heads, diagonal, score, vt_tile, None, *flash_refs)
    l_ref, acc_ref = flash_refs[-2:]
    for g in range(heads):
        o_ref[:, g * LANES:(g + 1) * LANES] = _flash_out(l_ref, acc_ref, g).astype(o_ref.dtype)


def mla_attention(qn, qr, kn, kr, vt, tq=512, heads=2):
    S, W = qn.shape
    tq = min(tq, S)
    gw = heads * LANES
    qspec = pl.BlockSpec((tq, gw), lambda h, i: (i, h))
    return pl.pallas_call(
        functools.partial(_mla_attn_kernel, tq=tq, heads=heads),
        grid=(W // gw, S // tq),
        in_specs=[qspec, qspec,
                  pl.BlockSpec((S, gw), lambda h, i: (0, h)),
                  pl.BlockSpec((S, LANES), lambda h, i: (0, 0)),
                  pl.BlockSpec((gw, S), lambda h, i: (h, 0))],
        out_specs=qspec,
        out_shape=jax.ShapeDtypeStruct((S, W), BF16),
        scratch_shapes=[pltpu.VMEM((heads, tq, 2 * LANES), BF16)] + _flash_scratch(heads, tq, tq),
        compiler_params=_params("parallel", "arbitrary"),
        name="mla_attention",
    )(qn, qr, kn, kr, vt)


def _proj_res_kernel(o_ref, w_ref, h_ref, gpost_ref, gnext_ref, hn_ref, xn_ref, mix_ref):
    o = o_ref[...]
    n = w_ref.shape[1]
    chunk = 4 * LANES
    for c in range(n // chunk):
        sl = slice(c * chunk, (c + 1) * chunk)
        mix_ref[:, sl] = _dot(o, w_ref[:, sl])
    mix = mix_ref[...]
    hn = h_ref[...] + mix * _rms_scale(mix) * gpost_ref[...]
    hn_ref[...] = hn
    xn_ref[...] = (hn * _rms_scale(hn) * gnext_ref[...]).astype(xn_ref.dtype)


def proj_res_norm(o, w, h, g_post, g_next, tm=256):
    S, K = o.shape
    D = w.shape[1]
    tm = min(tm, S)
    row = lambda n: pl.BlockSpec((tm, n), lambda i: (i, 0))
    gspec = pl.BlockSpec((1, D), lambda i: (0, 0))
    return pl.pallas_call(
        _proj_res_kernel,
        grid=(S // tm,),
        in_specs=[row(K), pl.BlockSpec((K, D), lambda i: (0, 0)), row(D), gspec, gspec],
        out_specs=[row(D), row(D)],
        out_shape=[jax.ShapeDtypeStruct((S, D), F32), jax.ShapeDtypeStruct((S, D), BF16)],
        scratch_shapes=[pltpu.VMEM((tm, D), F32)],
        compiler_params=_params("parallel"),
        name="proj_res_norm",
    )(o, w, h, g_post.reshape(1, D), g_next.reshape(1, D))


HALO = 16


def _gelu_tanh(x):
    return 0.5 * x * (1.0 + jnp.tanh(math.sqrt(2.0 / math.pi) * (x + 0.044715 * (x * x * x))))


def _ffn_kernel(xn_ref, halo_ref, wg_ref, wu_ref, cwg_ref, cwu_ref, cbg_ref, cbu_ref, wo_ref, h_ref,
                gpost_ref, gnext_ref, hn_ref, *rest, n_next):
    xn_next_refs = rest[:n_next]
    xcat_ref, yg_ref, yu_ref, acc_ref = rest[n_next:]
    i = pl.program_id(0)
    c = pl.program_id(1)
    tm = xn_ref.shape[0]

    @pl.when(c == 0)
    def _():
        halo = halo_ref[...]
        xcat_ref[0:HALO, :] = jnp.where(i == 0, jnp.zeros_like(halo), halo)
        xcat_ref[HALO:, :] = xn_ref[...]
        acc_ref[...] = jnp.zeros(acc_ref.shape, F32)

    xcat = xcat_ref[...]

    def conv(w_ref, y_ref, cw_ref, cb_ref):
        y_ref[...] = _dot(xcat, w_ref[...])
        cw = cw_ref[...]
        out = (y_ref[pl.ds(HALO - 2, tm), :] * cw[0:1] + y_ref[pl.ds(HALO - 1, tm), :] * cw[1:2]
               + y_ref[pl.ds(HALO, tm), :] * cw[2:3])
        return out + cb_ref[...]

    act = _gelu_tanh(conv(wg_ref, yg_ref, cwg_ref, cbg_ref)) * conv(wu_ref, yu_ref, cwu_ref, cbu_ref)
    acc_ref[...] += _dot(act.astype(BF16), wo_ref[...])

    @pl.when(c == pl.num_programs(1) - 1)
    def _():
        f = acc_ref[...]
        hn = h_ref[...] + f * _rms_scale(f) * gpost_ref[...]
        hn_ref[...] = hn
        if n_next:
            y = hn * _rms_scale(hn)
            for k in range(n_next):
                xn_next_refs[k][...] = (y * gnext_ref[k:k + 1, :]).astype(BF16)


def ffn(xn, w_in, conv_w, conv_b, w_out, h, g_post, g_next, tm=512, tf=512):
    S, D = xn.shape
    FF = w_out.shape[0]
    tm = min(tm, S)
    nf = FF // tf
    n_next = 0 if g_next is None else g_next.shape[0]
    if g_next is None:
        g_next = jnp.ones((1, D), F32)
    cb = conv_b.reshape(1, 2 * FF)
    hb = tm // HALO
    row = pl.BlockSpec((tm, D), lambda i, c: (i, 0))
    in_specs = [
        row,
        pl.BlockSpec((HALO, D), lambda i, c: (jnp.maximum(i * hb - 1, 0), 0)),
        pl.BlockSpec((D, tf), lambda i, c: (0, c)),
        pl.BlockSpec((D, tf), lambda i, c: (0, c + nf)),
        pl.BlockSpec((CONV_WIDTH, tf), lambda i, c: (0, c)),
        pl.BlockSpec((CONV_WIDTH, tf), lambda i, c: (0, c + nf)),
        pl.BlockSpec((1, tf), lambda i, c: (0, c)),
        pl.BlockSpec((1, tf), lambda i, c: (0, c + nf)),
        pl.BlockSpec((tf, D), lambda i, c: (c, 0)),
        row,
        pl.BlockSpec((1, D), lambda i, c: (0, 0)),
        pl.BlockSpec(g_next.shape, lambda i, c: (0, 0)),
    ]
    outs = pl.pallas_call(
        functools.partial(_ffn_kernel, n_next=n_next),
        grid=(S // tm, nf),
        in_specs=in_specs,
        out_specs=[row] * (1 + n_next),
        out_shape=[jax.ShapeDtypeStruct((S, D), F32)] + [jax.ShapeDtypeStruct((S, D), BF16)] * n_next,
        scratch_shapes=[pltpu.VMEM((HALO + tm, D), BF16), pltpu.VMEM((HALO + tm, tf), F32),
                        pltpu.VMEM((HALO + tm, tf), F32), pltpu.VMEM((tm, D), F32)],
        compiler_params=_params("parallel", "arbitrary"),
        name="conv_glu_ffn",
    )(xn, xn, w_in, w_in, conv_w, conv_w, cb, cb, w_out, h, g_post.reshape(1, D), g_next)
    return outs[0], list(outs[1:])


def _matmul_kernel(x_ref, w_ref, o_ref, *mean_ref, scale):
    y = _dot(x_ref[...], w_ref[...])
    if scale != 1.0:
        y = y * scale
    o_ref[...] = y.astype(o_ref.dtype)
    if mean_ref:
        mean_ref[0][...] = jnp.mean(y, axis=0, keepdims=True)[None]


def matmul(x, w, scale=1.0, tm=512, tn=1024, with_mean=False):
    S, K = x.shape
    N = w.shape[1]
    tm = min(tm, S)
    out_specs = [pl.BlockSpec((tm, tn), lambda n, i: (i, n))]
    out_shape = [jax.ShapeDtypeStruct((S, N), BF16)]
    if with_mean:
        out_specs.append(pl.BlockSpec((1, 1, tn), lambda n, i: (i, 0, n)))
        out_shape.append(jax.ShapeDtypeStruct((S // tm, 1, N), F32))
    outs = pl.pallas_call(
        functools.partial(_matmul_kernel, scale=scale),
        grid=(N // tn, S // tm),
        in_specs=[pl.BlockSpec((tm, K), lambda n, i: (i, 0)), pl.BlockSpec((K, tn), lambda n, i: (0, n))],
        out_specs=out_specs,
        out_shape=out_shape,
        compiler_params=_params("parallel", "parallel"),
        name="matmul_mean" if with_mean else "matmul",
    )(x, w)
    return outs if with_mean else outs[0]


def _matmul_nt_kernel(a_ref, b_ref, o_ref):
    o_ref[...] = _dot_nt(a_ref[...], b_ref[...]).astype(o_ref.dtype)


def matmul_nt(a, b, tm=1024, tn=512):
    M, K = a.shape
    N = b.shape[0]
    tm, tn = min(tm, M), min(tn, N)
    return pl.pallas_call(
        _matmul_nt_kernel,
        grid=(M // tm, N // tn),
        in_specs=[pl.BlockSpec((tm, K), lambda m, n: (m, 0)), pl.BlockSpec((tn, K), lambda m, n: (n, 0))],
        out_specs=pl.BlockSpec((tm, tn), lambda m, n: (m, n)),
        out_shape=jax.ShapeDtypeStruct((M, N), BF16),
        compiler_params=_params("parallel", "parallel"),
        name="matmul_nt",
    )(a, b)


def _rel_bucket(dist):
    n = jnp.maximum(dist, 0)
    max_exact = REL_BUCKETS // 2
    nf = jnp.maximum(n, 1).astype(F32)
    large = max_exact + (jnp.log(nf / max_exact) / math.log(REL_MAX_DIST / max_exact)
                         * (REL_BUCKETS - max_exact)).astype(jnp.int32)
    large = jnp.minimum(large, REL_BUCKETS - 1)
    return jnp.where(n < max_exact, n, large)


def _bias_lookup(bucket, tbl_row):
    rows, cols = bucket.shape
    tb = jnp.broadcast_to(tbl_row, (rows, LANES))
    parts = [jnp.take_along_axis(tb, bucket[:, c * LANES:(c + 1) * LANES], axis=1) for c in range(cols // LANES)]
    return jnp.concatenate(parts, axis=1)


def _moba_bias_kernel(posq_ref, pk0_ref, pk1_ref, tbl_ref, o_ref):
    L = MOBA_BLOCK
    posq = posq_ref[0]
    for half, pk_ref in enumerate((pk0_ref, pk1_ref)):
        bucket = _rel_bucket(posq - pk_ref[...])
        for h in range(o_ref.shape[1]):
            o_ref[0, h, half * L:(half + 1) * L, :] = _bias_lookup(bucket, tbl_ref[h:h + 1, :])


def moba_bias_tiles(pos_blk, pos_col, tbl):
    NB, L = pos_blk.shape
    H = tbl.shape[0]
    near0 = lambda i: jnp.maximum(i - 1, 0)
    return pl.pallas_call(
        _moba_bias_kernel,
        grid=(NB,),
        in_specs=[pl.BlockSpec((1, 1, L), lambda i: (i, 0, 0)),
                  pl.BlockSpec((L, 1), lambda i: (near0(i), 0)),
                  pl.BlockSpec((L, 1), lambda i: (near0(i) + 1, 0)),
                  pl.BlockSpec(tbl.shape, lambda i: (0, 0))],
        out_specs=pl.BlockSpec((1, H, 2 * L, L), lambda i: (i, 0, 0, 0)),
        out_shape=jax.ShapeDtypeStruct((NB, H, 2 * L, L), F32),
        compiler_params=_params("parallel"),
        name="moba_bias_tiles",
    )(pos_blk.reshape(NB, 1, L), pos_col, pos_col, tbl)


def _moba_kernel(qmin_ref, kmax_ref, q_ref, k_ref, vt_ref, kmean_ref, bias_ref, posq_ref, posk_ref, tbl_ref,
                 o_ref, qaug_ref, qnear_ref, *flash_refs, heads, nb):
    hg = pl.program_id(0)
    i = pl.program_id(1)
    L = MOBA_BLOCK
    P = 2 * L
    n = i // 2
    b0 = jnp.maximum(i - 1, 0)
    nbp = min(LANES, -(-nb // 8) * 8)
    blk = lax.broadcasted_iota(jnp.int32, (nbp, L), 0)
    for g in range(heads):
        hs = slice(g * LANES, (g + 1) * LANES)
        q = q_ref[:, hs]
        gate = _dot_nt(kmean_ref[:, hs].astype(BF16), q)[:nbp]
        gate = jnp.where(blk < i, gate, -jnp.inf)
        keep = jnp.zeros((nbp, L), F32)
        for _ in range(MOBA_TOPK):
            mx = jnp.max(gate, axis=0, keepdims=True)
            first = jnp.min(jnp.where(gate == mx, blk, nbp), axis=0, keepdims=True)
            hit = blk == first
            keep = jnp.where(hit, 1.0, keep)
            gate = jnp.where(hit, -jnp.inf, gate)
        keep = jnp.where(blk < i, keep, jnp.where(blk == i, 1.0, 0.0))
        pen = jnp.where(keep > 0.0, 0.0, NEG)
        pen_sweep = jnp.where(blk >= b0, NEG, pen)
        if nbp < LANES:
            pad = jnp.zeros((LANES - nbp, L), F32)
            pen = jnp.concatenate([pen, pad], axis=0)
            pen_sweep = jnp.concatenate([pen_sweep, pad], axis=0)
        qnear_ref[g, :, 0:LANES] = q
        qnear_ref[g, :, LANES:] = pen.T.astype(BF16)
        qaug_ref[g, :, 0:LANES] = q
        qaug_ref[g, :, LANES:] = pen_sweep.T.astype(BF16)

    lane = lax.broadcasted_iota(jnp.int32, (P, LANES), 1)
    half = jnp.where(lax.broadcasted_iota(jnp.int32, (P, LANES), 0) >= L, 1, 0)

    def rows(t):
        return pl.ds(pl.multiple_of(jnp.where(t == 0, b0 * L, (t - 1) * P), L), P)

    def raw_score(t, g, first_blk, qa_ref):
        onehot = jnp.where(lane == first_blk + half, 1.0, 0.0).astype(BF16)
        kaug = jnp.concatenate([k_ref[rows(t), g * LANES:(g + 1) * LANES], onehot], axis=1)
        return _dot_nt(kaug, qa_ref[g])

    def score(t, g):
        return raw_score(t, g, 2 * (t - 1), qaug_ref)

    def vt_tile(t, g):
        return vt_ref[g * LANES:(g + 1) * LANES, rows(t)]

    def first_score(g):
        kidx = b0 * L + lax.broadcasted_iota(jnp.int32, (P, L), 0)
        qidx = i * L + lax.broadcasted_iota(jnp.int32, (P, L), 1)
        return jnp.where(kidx <= qidx, raw_score(0, g, b0, qnear_ref) + bias_ref[0, g], NEG)

    def near(t):
        def blk_near(b):
            return jnp.logical_and(b < b0, qmin_ref[i] - kmax_ref[b] < FAR_DIST)
        return jnp.logical_or(blk_near(2 * (t - 1)), blk_near(2 * (t - 1) + 1))

    _flash_pipeline(n, heads, first_score, score, vt_tile,
                    lambda t: jnp.where(jnp.logical_and(t > 0, near(jnp.maximum(t, 1))), -NEG, 0.0), *flash_refs)
    m_ref, l_ref, acc_ref = flash_refs[-3:]

    def redo(t, carry):
        @pl.when(near(t))
        def _():
            posk = jnp.broadcast_to(posk_ref[t - 1], (LANES, P)).T[:, 0:1]
            bucket = _rel_bucket(posq_ref[0] - posk)
            for g in range(heads):
                s = score(t, g) + _bias_lookup(bucket, tbl_ref[pl.ds(hg * heads + g, 1), :])
                p, alpha = _flash_softmax(s, m_ref, l_ref, g)
                _flash_accumulate(vt_tile(t, g), p, alpha, acc_ref, g)
        return carry

    lax.fori_loop(1, n + 1, redo, 0)
    for g in range(heads):
        o_ref[:, g * LANES:(g + 1) * LANES] = _flash_out(l_ref, acc_ref, g).astype(o_ref.dtype)


def moba_attention(q, k, vt, kmean_p, bias_t, pos_blk, tbl, qmin, kmax, heads=4):
    S, W = q.shape
    L = MOBA_BLOCK
    NB = S // L
    assert NB % 2 == 0
    gw = heads * LANES
    qspec = pl.BlockSpec((L, gw), lambda h, i, *_: (i, h))
    pos_pair = pos_blk.reshape(NB // 2, 1, 2 * L)
    grid_spec = pltpu.PrefetchScalarGridSpec(
        num_scalar_prefetch=2,
        grid=(W // gw, NB),
        in_specs=[
            qspec,
            pl.BlockSpec((S, gw), lambda h, i, *_: (0, h)),
            pl.BlockSpec((gw, S), lambda h, i, *_: (h, 0)),
            pl.BlockSpec((kmean_p.shape[0], gw), lambda h, i, *_: (0, h)),
            pl.BlockSpec((1, heads, 2 * L, L), lambda h, i, *_: (i, h, 0, 0)),
            pl.BlockSpec((1, 1, L), lambda h, i, *_: (i, 0, 0)),
            pl.BlockSpec(pos_pair.shape, lambda h, i, *_: (0, 0, 0)),
            pl.BlockSpec(tbl.shape, lambda h, i, *_: (0, 0)),
        ],
        out_specs=qspec,
        scratch_shapes=[pltpu.VMEM((heads, L, 2 * LANES), BF16)] * 2 + _flash_scratch(heads, 2 * L, L),
    )
    return pl.pallas_call(
        functools.partial(_moba_kernel, heads=heads, nb=NB),
        grid_spec=grid_spec,
        out_shape=jax.ShapeDtypeStruct((S, W), BF16),
        compiler_params=_params("parallel", "arbitrary"),
        name="moba_attention",
    )(qmin, kmax, q, k, vt, kmean_p, bias_t, pos_blk.reshape(NB, 1, L), pos_pair, tbl)


def _rope_lanes(w):
    half = QK_ROPE // 2
    z = jnp.zeros(w.shape[:-1] + (half,), w.dtype)
    return jnp.concatenate([w[..., :half], z, w[..., half:], z], axis=-1)


def _prep_mla(w_in, w_q_up, w_kv_up):
    w1 = jnp.concatenate([w_in[:, :Q_LORA + KV_LORA], _rope_lanes(w_in[:, Q_LORA + KV_LORA:])], axis=1)
    wq = w_q_up.reshape(Q_LORA, MLA_HEADS, QK_NOPE + QK_ROPE)
    wqn = wq[:, :, :QK_NOPE].reshape(Q_LORA, MLA_HEADS * QK_NOPE)
    wqr = _rope_lanes(wq[:, :, QK_NOPE:]).reshape(Q_LORA, MLA_HEADS * LANES)
    wkv = w_kv_up.reshape(KV_LORA, MLA_HEADS, QK_NOPE + V_HEAD)
    wkn = wkv[:, :, :QK_NOPE].reshape(KV_LORA, -1)
    wvt = wkv[:, :, QK_NOPE:].reshape(KV_LORA, -1).T
    return [w.astype(BF16) for w in (w1, wqn, wqr, wkn, wvt)]


def kernel(x, positions, norm_gains, a_w_in, a_q_norm, a_w_q_up, a_kv_norm, a_w_kv_up, a_w_o, b_kv_norm, b_w_kv,
           b_w_q, b_w_o, rel_bias, ffn_w_in, ffn_conv_w, ffn_conv_b, ffn_w_out):
    B, S, D = x.shape
    depth = norm_gains.shape[0]
    n_a = a_w_in.shape[0]
    L = MOBA_BLOCK
    NB = S // L
    HW = MOBA_HEADS * MOBA_HEAD
    outs = []
    for b in range(B):
        pos = positions[b]
        pos_col = pos.reshape(S, 1)
        pos_blk = pos.reshape(NB, L)
        qmin = jnp.min(pos_blk, axis=1)
        kmax = jnp.max(pos_blk, axis=1)
        cos_t, sin_t = rope_tables(pos_col)
        tbl = (rel_bias - rel_bias[REL_BUCKETS - 1:REL_BUCKETS, :]).T * LOG2E
        tbl = jnp.pad(tbl, ((0, 0), (0, LANES - REL_BUCKETS)))
        bias_t = moba_bias_tiles(pos_blk, pos_col, tbl) if depth > n_a else None

        h = x[b]
        xn = norm_cast(h, norm_gains[0, 0])
        k = vt = kmean_p = None
        for layer in range(depth):
            g = norm_gains[layer]
            if layer < n_a:
                w1, wqn, wqr, wkn, wvt = _prep_mla(a_w_in[layer], a_w_q_up[layer], a_w_kv_up[layer])
                q_scale = (QK_NOPE + QK_ROPE) ** -0.5 * LOG2E
                qn, qr, kn, kr, vt_a = mla_proj(xn, w1, a_q_norm[layer].reshape(1, -1),
                                                a_kv_norm[layer].reshape(1, -1), wqn, wqr, wkn, wvt,
                                                cos_t, sin_t, q_scale)
                o = mla_attention(qn, qr, kn, kr, vt_a)
                w_o = a_w_o[layer]
            else:
                j = layer - n_a
                q = matmul(xn, b_w_q[j].astype(BF16), scale=MOBA_HEAD ** -0.5 * LOG2E)
                o = moba_attention(q, k, vt, kmean_p, bias_t, pos_blk, tbl, qmin, kmax)
                w_o = b_w_o[j]
            h, xn = proj_res_norm(o, w_o.astype(BF16), h, g[1], g[2])
            if layer + 1 == depth:
                g_next = None
            elif layer + 1 == n_a:
                g_next = jnp.stack([norm_gains[layer + 1, 0], b_kv_norm])
            else:
                g_next = norm_gains[layer + 1, 0].reshape(1, D)
            h, nxt = ffn(xn, ffn_w_in[layer].astype(BF16), ffn_conv_w[layer], ffn_conv_b[layer],
                         ffn_w_out[layer].astype(BF16), h, g[3], g_next)
            if nxt:
                xn = nxt[0]
            if layer + 1 == n_a:
                k, kmean = matmul(nxt[1], b_w_kv[:, :HW].astype(BF16), tm=L, with_mean=True)
                vt = matmul_nt(b_w_kv[:, HW:].T.astype(BF16), nxt[1])
                kmean_p = jnp.pad(kmean.reshape(NB, HW), ((0, LANES - NB), (0, 0)))
        outs.append(h)
    return jnp.stack(outs)
```

```python
import functools
import math

import jax
import jax.numpy as jnp
from jax import lax
from jax.experimental import pallas as pl
from jax.experimental.pallas import tpu as pltpu

F32 = jnp.float32
BF16 = jnp.bfloat16

MLA_HEADS = 16
Q_LORA = 512
KV_LORA = 512
QK_NOPE = 128
QK_ROPE = 64
V_HEAD = 128
ROPE_THETA = 10000.0
MOBA_HEADS = 16
MOBA_HEAD = 128
MOBA_BLOCK = 256
MOBA_TOPK = 3
REL_BUCKETS = 32
REL_MAX_DIST = 128
CONV_WIDTH = 3
EPS = 1e-6
NEG = -1e30
LOG2E = 1.4426950408889634

LANES = 128
FAR_DIST = REL_MAX_DIST
VMEM_LIMIT = 56 * 1024 * 1024


def _params(*sem):
    return pltpu.CompilerParams(dimension_semantics=sem, vmem_limit_bytes=VMEM_LIMIT)


def _rms_scale(x):
    return lax.rsqrt(jnp.mean(x * x, axis=-1, keepdims=True) + EPS)


def _dot(a, b):
    return jnp.dot(a, b, preferred_element_type=F32)


def _dot_nt(a, b):
    return lax.dot_general(a, b, (((1,), (1,)), ((), ())), preferred_element_type=F32)


def _norm_kernel(x_ref, g_ref, o_ref):
    x = x_ref[...]
    o_ref[...] = (x * _rms_scale(x) * g_ref[...]).astype(o_ref.dtype)


def norm_cast(x, g, tm=512):
    S, D = x.shape
    tm = min(tm, S)
    return pl.pallas_call(
        _norm_kernel,
        grid=(S // tm,),
        in_specs=[pl.BlockSpec((tm, D), lambda i: (i, 0)), pl.BlockSpec((1, D), lambda i: (0, 0))],
        out_specs=pl.BlockSpec((tm, D), lambda i: (i, 0)),
        out_shape=jax.ShapeDtypeStruct((S, D), BF16),
        compiler_params=_params("parallel"),
        name="norm_cast",
    )(x, g.reshape(1, D))


def _rope_table_kernel(pos_ref, inv_ref, sgn_ref, cos_ref, sin_ref):
    ang = pos_ref[...].astype(F32) * inv_ref[...]
    cos_ref[...] = jnp.cos(ang)
    sin_ref[...] = jnp.sin(ang) * sgn_ref[...]


def rope_tables(pos_col, tm=1024):
    S = pos_col.shape[0]
    tm = min(tm, S)
    half = QK_ROPE // 2
    inv = ROPE_THETA ** (-jnp.arange(half, dtype=F32) / half)
    z = jnp.zeros((half,), F32)
    inv_pat = jnp.concatenate([inv, z, inv, z]).reshape(1, LANES)
    o = jnp.ones((2 * half,), F32)
    sgn = jnp.concatenate([-o, o]).reshape(1, LANES)
    return pl.pallas_call(
        _rope_table_kernel,
        grid=(S // tm,),
        in_specs=[pl.BlockSpec((tm, 1), lambda i: (i, 0)),
                  pl.BlockSpec((1, LANES), lambda i: (0, 0)),
                  pl.BlockSpec((1, LANES), lambda i: (0, 0))],
        out_specs=[pl.BlockSpec((tm, LANES), lambda i: (i, 0))] * 2,
        out_shape=[jax.ShapeDtypeStruct((S, LANES), F32)] * 2,
        compiler_params=_params("parallel"),
        name="rope_tables",
    )(pos_col, inv_pat, sgn)


def _mla_proj_kernel(xn_ref, w1_ref, gq_ref, gkv_ref, wqn_ref, wqr_ref, wkn_ref, wvt_ref, cos_ref, sin_ref,
                     qn_ref, qr_ref, kn_ref, kr_ref, vt_ref, *, q_scale):
    xn = xn_ref[...]
    cos = cos_ref[...]
    sin = sin_ref[...]

    def rope(x):
        return x * cos + pltpu.roll(x, LANES // 2, axis=1) * sin

    cq = _dot(xn, w1_ref[:, 0:Q_LORA])
    ckv = _dot(xn, w1_ref[:, Q_LORA:Q_LORA + KV_LORA])
    kr = _dot(xn, w1_ref[:, Q_LORA + KV_LORA:])
    kr_ref[...] = rope(kr).astype(BF16)
    cqn = (cq * _rms_scale(cq) * gq_ref[...]).astype(BF16)
    ckvn = (ckv * _rms_scale(ckv) * gkv_ref[...]).astype(BF16)
    width = MLA_HEADS * QK_NOPE
    chunk = 4 * LANES
    for c in range(width // chunk):
        sl = slice(c * chunk, (c + 1) * chunk)
        qn_ref[:, sl] = (_dot(cqn, wqn_ref[:, sl]) * q_scale).astype(BF16)
        qr = _dot(cqn, wqr_ref[:, sl])
        for hh in range(chunk // LANES):
            x = qr[:, hh * LANES:(hh + 1) * LANES]
            lo = c * chunk + hh * LANES
            qr_ref[:, lo:lo + LANES] = (rope(x) * q_scale).astype(BF16)
        kn_ref[:, sl] = _dot(ckvn, wkn_ref[:, sl]).astype(BF16)
        vt = _dot_nt(wvt_ref[sl, :], ckvn).astype(BF16)
        hpc = chunk // LANES
        vt_ref[c * hpc * VT_ROWS:(c + 1) * hpc * VT_ROWS, :] = _with_ones_rows(vt, hpc)


def mla_proj(xn, w1, gq, gkv, wqn, wqr, wkn, wvt, cos_t, sin_t, q_scale, tm=256):
    S, D = xn.shape
    tm = min(tm, S)
    W = MLA_HEADS * QK_NOPE
    full = lambda a: pl.BlockSpec(a.shape, lambda i: (0, 0))
    row = lambda n: pl.BlockSpec((tm, n), lambda i: (i, 0))
    return pl.pallas_call(
        functools.partial(_mla_proj_kernel, q_scale=q_scale),
        grid=(S // tm,),
        in_specs=[row(D), full(w1), full(gq), full(gkv), full(wqn), full(wqr), full(wkn), full(wvt),
                  row(LANES), row(LANES)],
        out_specs=[row(W), row(W), row(W), row(LANES), pl.BlockSpec((MLA_HEADS * VT_ROWS, tm), lambda i: (0, i))],
        out_shape=[jax.ShapeDtypeStruct((S, W), BF16), jax.ShapeDtypeStruct((S, W), BF16),
                   jax.ShapeDtypeStruct((S, W), BF16), jax.ShapeDtypeStruct((S, LANES), BF16),
                   jax.ShapeDtypeStruct((MLA_HEADS * VT_ROWS, S), BF16)],
        compiler_params=_params("parallel"),
        name="mla_proj",
    )(xn, w1, gq, gkv, wqn, wqr, wkn, wvt, cos_t, sin_t)


ONES_ROWS = 16
VT_ROWS = LANES + ONES_ROWS


def _flash_init(m_ref, acc_ref, g):
    m_ref[g] = jnp.full(m_ref.shape[1:], NEG, F32)
    acc_ref[g] = jnp.zeros(acc_ref.shape[1:], F32)


def _flash_softmax(st, m_ref, g, off=None):
    m_prev = m_ref[g]
    m_new = jnp.maximum(m_prev, jnp.max(st, axis=0, keepdims=True))
    alpha = jnp.exp2(m_prev - m_new)
    p = jnp.exp2(st - (m_new if off is None else m_new + off))
    m_ref[g] = m_new
    return p.astype(BF16), alpha


def _flash_accumulate(vt, p, alpha, acc_ref, g):
    acc_ref[g] = alpha * acc_ref[g] + _dot(vt, p)


def _flash_out(acc_ref, g):
    acc = acc_ref[g]
    return (acc[0:LANES] / acc[LANES:LANES + 1]).T


def _with_ones_rows(vt, heads):
    ones = jnp.ones((ONES_ROWS, vt.shape[1]), vt.dtype)
    parts = []
    for h in range(heads):
        parts += [vt[h * LANES:(h + 1) * LANES], ones]
    return jnp.concatenate(parts, axis=0)


def _flash_pipeline(n, heads, first_score, score, vt_tile, off, s_ref, p_ref, a_ref, m_ref, acc_ref):
    for g in range(heads):
        _flash_init(m_ref, acc_ref, g)
        p_ref[g, 1] = jnp.zeros(p_ref.shape[2:], BF16)
        a_ref[g, 1] = jnp.ones(a_ref.shape[2:], F32)
        s_ref[g, 0] = first_score(g)

    def tick(t, a, b):
        for g in range(heads):
            s_ref[g, b] = score(t + 1, g)
        tp = jnp.maximum(t - 1, 0)
        for g in range(heads):
            _flash_accumulate(vt_tile(tp, g), p_ref[g, b], a_ref[g, b], acc_ref, g)
        o = None if off is None else off(t)
        for g in range(heads):
            p, alpha = _flash_softmax(s_ref[g, a], m_ref, g, o)
            p_ref[g, a] = p
            a_ref[g, a] = alpha

    def body(k, carry):
        tick(2 * k, 0, 1)
        tick(2 * k + 1, 1, 0)
        return carry

    lax.fori_loop(0, n // 2, body, 0)

    @pl.when(n % 2 == 1)
    def _():
        tick(n - 1, 0, 1)

    def last(e):
        tp = jnp.maximum(n - 1, 0)
        for g in range(heads):
            _flash_accumulate(vt_tile(tp, g), p_ref[g, 1 - e], a_ref[g, 1 - e], acc_ref, g)
        o = None if off is None else off(n)
        for g in range(heads):
            p, alpha = _flash_softmax(s_ref[g, e], m_ref, g, o)
            _flash_accumulate(vt_tile(n, g), p, alpha, acc_ref, g)

    @pl.when(n % 2 == 0)
    def _():
        last(0)

    @pl.when(n % 2 == 1)
    def _():
        last(1)


def _flash_scratch(heads, tk, tq):
    return [pltpu.VMEM((heads, 2, tk, tq), F32), pltpu.VMEM((heads, 2, tk, tq), BF16),
            pltpu.VMEM((heads, 2, 1, tq), F32), pltpu.VMEM((heads, 1, tq), F32),
            pltpu.VMEM((heads, VT_ROWS, tq), F32)]


def _mla_attn_kernel(qn_ref, qr_ref, kn_ref, kr_ref, vt_ref, o_ref, q_ref, *flash_refs, tq, heads):
    i = pl.program_id(1)
    for g in range(heads):
        hs = slice(g * LANES, (g + 1) * LANES)
        q_ref[g, :, 0:LANES] = qn_ref[:, hs]
        q_ref[g, :, LANES:] = qr_ref[:, hs]

    def rows(t):
        return pl.ds(pl.multiple_of(jnp.where(t == 0, i, t - 1) * tq, tq), tq)

    def score(t, g):
        k = jnp.concatenate([kn_ref[rows(t), g * LANES:(g + 1) * LANES], kr_ref[rows(t), :]], axis=1)
        return _dot_nt(k, q_ref[g])

    def vt_tile(t, g):
        return vt_ref[g * VT_ROWS:(g + 1) * VT_ROWS, rows(t)]

    def diagonal(g):
        kidx = lax.broadcasted_iota(jnp.int32, (tq, tq), 0)
        qidx = lax.broadcasted_iota(jnp.int32, (tq, tq), 1)
        return jnp.where(kidx <= qidx, score(0, g), NEG)

    _flash_pipeline(i, heads, diagonal, score, vt_tile, None, *flash_refs)
    acc_ref = flash_refs[-1]
    for g in range(heads):
        o_ref[:, g * LANES:(g + 1) * LANES] = _flash_out(acc_ref, g).astype(o_ref.dtype)


def mla_attention(qn, qr, kn, kr, vt, tq=512, heads=2):
    S, W = qn.shape
    tq = min(tq, S)
    gw = heads * LANES
    qspec = pl.BlockSpec((tq, gw), lambda h, i: (i, h))
    return pl.pallas_call(
        functools.partial(_mla_attn_kernel, tq=tq, heads=heads),
        grid=(W // gw, S // tq),
        in_specs=[qspec, qspec,
                  pl.BlockSpec((S, gw), lambda h, i: (0, h)),
                  pl.BlockSpec((S, LANES), lambda h, i: (0, 0)),
                  pl.BlockSpec((heads * VT_ROWS, S), lambda h, i: (h, 0))],
        out_specs=qspec,
        out_shape=jax.ShapeDtypeStruct((S, W), BF16),
        scratch_shapes=[pltpu.VMEM((heads, tq, 2 * LANES), BF16)] + _flash_scratch(heads, tq, tq),
        compiler_params=_params("parallel", "arbitrary"),
        name="mla_attention",
    )(qn, qr, kn, kr, vt)


def _proj_res_kernel(o_ref, w_ref, h_ref, gpost_ref, gnext_ref, hn_ref, xn_ref, mix_ref):
    o = o_ref[...]
    n = w_ref.shape[1]
    chunk = 4 * LANES
    for c in range(n // chunk):
        sl = slice(c * chunk, (c + 1) * chunk)
        mix_ref[:, sl] = _dot(o, w_ref[:, sl])
    mix = mix_ref[...]
    hn = h_ref[...] + mix * _rms_scale(mix) * gpost_ref[...]
    hn_ref[...] = hn
    xn_ref[...] = (hn * _rms_scale(hn) * gnext_ref[...]).astype(xn_ref.dtype)


def proj_res_norm(o, w, h, g_post, g_next, tm=256):
    S, K = o.shape
    D = w.shape[1]
    tm = min(tm, S)
    row = lambda n: pl.BlockSpec((tm, n), lambda i: (i, 0))
    gspec = pl.BlockSpec((1, D), lambda i: (0, 0))
    return pl.pallas_call(
        _proj_res_kernel,
        grid=(S // tm,),
        in_specs=[row(K), pl.BlockSpec((K, D), lambda i: (0, 0)), row(D), gspec, gspec],
        out_specs=[row(D), row(D)],
        out_shape=[jax.ShapeDtypeStruct((S, D), F32), jax.ShapeDtypeStruct((S, D), BF16)],
        scratch_shapes=[pltpu.VMEM((tm, D), F32)],
        compiler_params=_params("parallel"),
        name="proj_res_norm",
    )(o, w, h, g_post.reshape(1, D), g_next.reshape(1, D))


HALO = 16


def _gelu_tanh(x):
    return 0.5 * x * (1.0 + jnp.tanh(math.sqrt(2.0 / math.pi) * (x + 0.044715 * (x * x * x))))


def _ffn_kernel(xn_ref, halo_ref, wg_ref, wu_ref, cwg_ref, cwu_ref, cbg_ref, cbu_ref, wo_ref, h_ref,
                gpost_ref, gnext_ref, hn_ref, *rest, n_next):
    xn_next_refs = rest[:n_next]
    xcat_ref, yg_ref, yu_ref, acc_ref = rest[n_next:]
    i = pl.program_id(0)
    c = pl.program_id(1)
    tm = xn_ref.shape[0]

    @pl.when(c == 0)
    def _():
        halo = halo_ref[...]
        xcat_ref[0:HALO, :] = jnp.where(i == 0, jnp.zeros_like(halo), halo)
        xcat_ref[HALO:, :] = xn_ref[...]
        acc_ref[...] = jnp.zeros(acc_ref.shape, F32)

    xcat = xcat_ref[...]

    def conv(w_ref, y_ref, cw_ref, cb_ref):
        y_ref[...] = _dot(xcat, w_ref[...])
        cw = cw_ref[...]
        out = (y_ref[pl.ds(HALO - 2, tm), :] * cw[0:1] + y_ref[pl.ds(HALO - 1, tm), :] * cw[1:2]
               + y_ref[pl.ds(HALO, tm), :] * cw[2:3])
        return out + cb_ref[...]

    act = _gelu_tanh(conv(wg_ref, yg_ref, cwg_ref, cbg_ref)) * conv(wu_ref, yu_ref, cwu_ref, cbu_ref)
    acc_ref[...] += _dot(act.astype(BF16), wo_ref[...])

    @pl.when(c == pl.num_programs(1) - 1)
    def _():
        f = acc_ref[...]
        hn = h_ref[...] + f * _rms_scale(f) * gpost_ref[...]
        hn_ref[...] = hn
        if n_next:
            y = hn * _rms_scale(hn)
            for k in range(n_next):
                xn_next_refs[k][...] = (y * gnext_ref[k:k + 1, :]).astype(BF16)


def ffn(xn, layer, w_in, conv_w, conv_b, w_out, h, g_post, g_next, tm=512, tf=512):
    S, D = xn.shape
    FF = w_out.shape[1]
    tm = min(tm, S)
    nf = FF // tf
    n_next = 0 if g_next is None else g_next.shape[0]
    if g_next is None:
        g_next = jnp.ones((1, D), F32)
    cb = conv_b
    hb = tm // HALO
    row = pl.BlockSpec((tm, D), lambda i, c: (i, 0))
    in_specs = [
        row,
        pl.BlockSpec((HALO, D), lambda i, c: (jnp.maximum(i * hb - 1, 0), 0)),
        pl.BlockSpec((None, D, tf), lambda i, c: (layer, 0, c)),
        pl.BlockSpec((None, D, tf), lambda i, c: (layer, 0, c + nf)),
        pl.BlockSpec((None, CONV_WIDTH, tf), lambda i, c: (layer, 0, c)),
        pl.BlockSpec((None, CONV_WIDTH, tf), lambda i, c: (layer, 0, c + nf)),
        pl.BlockSpec((None, 1, tf), lambda i, c: (layer, 0, c)),
        pl.BlockSpec((None, 1, tf), lambda i, c: (layer, 0, c + nf)),
        pl.BlockSpec((None, tf, D), lambda i, c: (layer, c, 0)),
        row,
        pl.BlockSpec((1, D), lambda i, c: (0, 0)),
        pl.BlockSpec(g_next.shape, lambda i, c: (0, 0)),
    ]
    outs = pl.pallas_call(
        functools.partial(_ffn_kernel, n_next=n_next),
        grid=(S // tm, nf),
        in_specs=in_specs,
        out_specs=[row] * (1 + n_next),
        out_shape=[jax.ShapeDtypeStruct((S, D), F32)] + [jax.ShapeDtypeStruct((S, D), BF16)] * n_next,
        scratch_shapes=[pltpu.VMEM((HALO + tm, D), BF16), pltpu.VMEM((HALO + tm, tf), F32),
                        pltpu.VMEM((HALO + tm, tf), F32), pltpu.VMEM((tm, D), F32)],
        compiler_params=_params("parallel", "arbitrary"),
        name="conv_glu_ffn",
    )(xn, xn, w_in, w_in, conv_w, conv_w, cb, cb, w_out, h, g_post.reshape(1, D), g_next)
    return outs[0], list(outs[1:])


def _matmul_kernel(x_ref, w_ref, o_ref, *mean_ref, scale):
    y = _dot(x_ref[...], w_ref[...])
    if scale != 1.0:
        y = y * scale
    o_ref[...] = y.astype(o_ref.dtype)
    if mean_ref:
        mean_ref[0][...] = jnp.mean(y, axis=0, keepdims=True)[None]


def matmul(x, w, scale=1.0, tm=512, tn=1024, with_mean=False):
    S, K = x.shape
    N = w.shape[1]
    tm = min(tm, S)
    out_specs = [pl.BlockSpec((tm, tn), lambda n, i: (i, n))]
    out_shape = [jax.ShapeDtypeStruct((S, N), BF16)]
    if with_mean:
        out_specs.append(pl.BlockSpec((1, 1, tn), lambda n, i: (i, 0, n)))
        out_shape.append(jax.ShapeDtypeStruct((S // tm, 1, N), F32))
    outs = pl.pallas_call(
        functools.partial(_matmul_kernel, scale=scale),
        grid=(N // tn, S // tm),
        in_specs=[pl.BlockSpec((tm, K), lambda n, i: (i, 0)), pl.BlockSpec((K, tn), lambda n, i: (0, n))],
        out_specs=out_specs,
        out_shape=out_shape,
        compiler_params=_params("parallel", "parallel"),
        name="matmul_mean" if with_mean else "matmul",
    )(x, w)
    return outs if with_mean else outs[0]


def _matmul_nt_kernel(a_ref, b_ref, o_ref, *, heads):
    y = _dot_nt(a_ref[...], b_ref[...]).astype(o_ref.dtype)
    o_ref[...] = _with_ones_rows(y, heads) if heads else y


def matmul_nt(a, b, tm=1024, tn=512, ones_rows=False):
    M, K = a.shape
    N = b.shape[0]
    tm, tn = min(tm, M), min(tn, N)
    heads = tm // LANES if ones_rows else 0
    to = heads * VT_ROWS if ones_rows else tm
    return pl.pallas_call(
        functools.partial(_matmul_nt_kernel, heads=heads),
        grid=(M // tm, N // tn),
        in_specs=[pl.BlockSpec((tm, K), lambda m, n: (m, 0)), pl.BlockSpec((tn, K), lambda m, n: (n, 0))],
        out_specs=pl.BlockSpec((to, tn), lambda m, n: (m, n)),
        out_shape=jax.ShapeDtypeStruct((M // tm * to, N), BF16),
        compiler_params=_params("parallel", "parallel"),
        name="matmul_nt",
    )(a, b)


def _rel_bucket(dist):
    n = jnp.maximum(dist, 0)
    max_exact = REL_BUCKETS // 2
    nf = jnp.maximum(n, 1).astype(F32)
    large = max_exact + (jnp.log(nf / max_exact) / math.log(REL_MAX_DIST / max_exact)
                         * (REL_BUCKETS - max_exact)).astype(jnp.int32)
    large = jnp.minimum(large, REL_BUCKETS - 1)
    return jnp.where(n < max_exact, n, large)


def _bias_lookup(bucket, tbl_row):
    rows, cols = bucket.shape
    tb = jnp.broadcast_to(tbl_row, (rows, LANES))
    parts = [jnp.take_along_axis(tb, bucket[:, c * LANES:(c + 1) * LANES], axis=1) for c in range(cols // LANES)]
    return jnp.concatenate(parts, axis=1)


def _moba_bias_kernel(posq_ref, pk0_ref, pk1_ref, tbl_ref, o_ref):
    L = MOBA_BLOCK
    posq = posq_ref[0]
    for half, pk_ref in enumerate((pk0_ref, pk1_ref)):
        bucket = _rel_bucket(posq - pk_ref[...])
        for h in range(o_ref.shape[1]):
            o_ref[0, h, half * L:(half + 1) * L, :] = _bias_lookup(bucket, tbl_ref[h:h + 1, :])


def moba_bias_tiles(pos_blk, pos_col, tbl):
    NB, L = pos_blk.shape
    H = tbl.shape[0]
    near0 = lambda i: jnp.maximum(i - 1, 0)
    return pl.pallas_call(
        _moba_bias_kernel,
        grid=(NB,),
        in_specs=[pl.BlockSpec((1, 1, L), lambda i: (i, 0, 0)),
                  pl.BlockSpec((L, 1), lambda i: (near0(i), 0)),
                  pl.BlockSpec((L, 1), lambda i: (near0(i) + 1, 0)),
                  pl.BlockSpec(tbl.shape, lambda i: (0, 0))],
        out_specs=pl.BlockSpec((1, H, 2 * L, L), lambda i: (i, 0, 0, 0)),
        out_shape=jax.ShapeDtypeStruct((NB, H, 2 * L, L), F32),
        compiler_params=_params("parallel"),
        name="moba_bias_tiles",
    )(pos_blk.reshape(NB, 1, L), pos_col, pos_col, tbl)


def _moba_kernel(qmin_ref, kmax_ref, q_ref, k_ref, vt_ref, kmean_ref, bias_ref, posq_ref, posk_ref, tbl_ref,
                 o_ref, qaug_ref, qnear_ref, *flash_refs, heads, nb):
    hg = pl.program_id(0)
    i = pl.program_id(1)
    L = MOBA_BLOCK
    P = 2 * L
    n = i // 2
    b0 = jnp.maximum(i - 1, 0)
    nbp = min(LANES, -(-nb // 8) * 8)
    blk = lax.broadcasted_iota(jnp.int32, (nbp, L), 0)
    for g in range(heads):
        hs = slice(g * LANES, (g + 1) * LANES)
        q = q_ref[:, hs]
        gate = _dot_nt(kmean_ref[:, hs].astype(BF16), q)[:nbp]
        gate = jnp.where(blk < i, gate, -jnp.inf)
        keep = jnp.zeros((nbp, L), F32)
        for _ in range(MOBA_TOPK):
            mx = jnp.max(gate, axis=0, keepdims=True)
            first = jnp.min(jnp.where(gate == mx, blk, nbp), axis=0, keepdims=True)
            hit = blk == first
            keep = jnp.where(hit, 1.0, keep)
            gate = jnp.where(hit, -jnp.inf, gate)
        keep = jnp.where(blk < i, keep, jnp.where(blk == i, 1.0, 0.0))
        pen = jnp.where(keep > 0.0, 0.0, NEG)
        pen_sweep = jnp.where(blk >= b0, NEG, pen)
        if nbp < LANES:
            pad = jnp.zeros((LANES - nbp, L), F32)
            pen = jnp.concatenate([pen, pad], axis=0)
            pen_sweep = jnp.concatenate([pen_sweep, pad], axis=0)
        qnear_ref[g, :, 0:LANES] = q
        qnear_ref[g, :, LANES:] = pen.T.astype(BF16)
        qaug_ref[g, :, 0:LANES] = q
        qaug_ref[g, :, LANES:] = pen_sweep.T.astype(BF16)

    lane = lax.broadcasted_iota(jnp.int32, (P, LANES), 1)
    half = jnp.where(lax.broadcasted_iota(jnp.int32, (P, LANES), 0) >= L, 1, 0)

    def rows(t):
        return pl.ds(pl.multiple_of(jnp.where(t == 0, b0 * L, (t - 1) * P), L), P)

    def raw_score(t, g, first_blk, qa_ref):
        onehot = jnp.where(lane == first_blk + half, 1.0, 0.0).astype(BF16)
        kaug = jnp.concatenate([k_ref[rows(t), g * LANES:(g + 1) * LANES], onehot], axis=1)
        return _dot_nt(kaug, qa_ref[g])

    def score(t, g):
        return raw_score(t, g, 2 * (t - 1), qaug_ref)

    def vt_tile(t, g):
        return vt_ref[g * VT_ROWS:(g + 1) * VT_ROWS, rows(t)]

    def first_score(g):
        kidx = b0 * L + lax.broadcasted_iota(jnp.int32, (P, L), 0)
        qidx = i * L + lax.broadcasted_iota(jnp.int32, (P, L), 1)
        return jnp.where(kidx <= qidx, raw_score(0, g, b0, qnear_ref) + bias_ref[0, g], NEG)

    def near(t):
        def blk_near(b):
            return jnp.logical_and(b < b0, qmin_ref[i] - kmax_ref[b] < FAR_DIST)
        return jnp.logical_or(blk_near(2 * (t - 1)), blk_near(2 * (t - 1) + 1))

    _flash_pipeline(n, heads, first_score, score, vt_tile,
                    lambda t: jnp.where(jnp.logical_and(t > 0, near(jnp.maximum(t, 1))), -NEG, 0.0), *flash_refs)
    m_ref, acc_ref = flash_refs[-2:]

    def redo(t, carry):
        @pl.when(near(t))
        def _():
            posk = jnp.broadcast_to(posk_ref[t - 1], (LANES, P)).T[:, 0:1]
            bucket = _rel_bucket(posq_ref[0] - posk)
            for g in range(heads):
                s = score(t, g) + _bias_lookup(bucket, tbl_ref[pl.ds(hg * heads + g, 1), :])
                p, alpha = _flash_softmax(s, m_ref, g)
                _flash_accumulate(vt_tile(t, g), p, alpha, acc_ref, g)
        return carry

    lax.fori_loop(1, n + 1, redo, 0)
    for g in range(heads):
        o_ref[:, g * LANES:(g + 1) * LANES] = _flash_out(acc_ref, g).astype(o_ref.dtype)


def moba_attention(q, k, vt, kmean_p, bias_t, pos_blk, tbl, qmin, kmax, heads=4):
    S, W = q.shape
    L = MOBA_BLOCK
    NB = S // L
    assert NB % 2 == 0
    gw = heads * LANES
    qspec = pl.BlockSpec((L, gw), lambda h, i, *_: (i, h))
    pos_pair = pos_blk.reshape(NB // 2, 1, 2 * L)
    grid_spec = pltpu.PrefetchScalarGridSpec(
        num_scalar_prefetch=2,
        grid=(W // gw, NB),
        in_specs=[
            qspec,
            pl.BlockSpec((S, gw), lambda h, i, *_: (0, h)),
            pl.BlockSpec((heads * VT_ROWS, S), lambda h, i, *_: (h, 0)),
            pl.BlockSpec((kmean_p.shape[0], gw), lambda h, i, *_: (0, h)),
            pl.BlockSpec((1, heads, 2 * L, L), lambda h, i, *_: (i, h, 0, 0)),
            pl.BlockSpec((1, 1, L), lambda h, i, *_: (i, 0, 0)),
            pl.BlockSpec(pos_pair.shape, lambda h, i, *_: (0, 0, 0)),
            pl.BlockSpec(tbl.shape, lambda h, i, *_: (0, 0)),
        ],
        out_specs=qspec,
        scratch_shapes=[pltpu.VMEM((heads, L, 2 * LANES), BF16)] * 2 + _flash_scratch(heads, 2 * L, L),
    )
    return pl.pallas_call(
        functools.partial(_moba_kernel, heads=heads, nb=NB),
        grid_spec=grid_spec,
        out_shape=jax.ShapeDtypeStruct((S, W), BF16),
        compiler_params=_params("parallel", "arbitrary"),
        name="moba_attention",
    )(qmin, kmax, q, k, vt, kmean_p, bias_t, pos_blk.reshape(NB, 1, L), pos_pair, tbl)


def _rope_lanes(w):
    half = QK_ROPE // 2
    z = jnp.zeros(w.shape[:-1] + (half,), w.dtype)
    return jnp.concatenate([w[..., :half], z, w[..., half:], z], axis=-1)


def _prep_mla(w_in, w_q_up, w_kv_up):
    w1 = jnp.concatenate([w_in[:, :Q_LORA + KV_LORA], _rope_lanes(w_in[:, Q_LORA + KV_LORA:])], axis=1)
    wq = w_q_up.reshape(Q_LORA, MLA_HEADS, QK_NOPE + QK_ROPE)
    wqn = wq[:, :, :QK_NOPE].reshape(Q_LORA, MLA_HEADS * QK_NOPE)
    wqr = _rope_lanes(wq[:, :, QK_NOPE:]).reshape(Q_LORA, MLA_HEADS * LANES)
    wkv = w_kv_up.reshape(KV_LORA, MLA_HEADS, QK_NOPE + V_HEAD)
    wkn = wkv[:, :, :QK_NOPE].reshape(KV_LORA, -1)
    wvt = wkv[:, :, QK_NOPE:].reshape(KV_LORA, -1).T
    return [w.astype(BF16) for w in (w1, wqn, wqr, wkn, wvt)]


def kernel(x, positions, norm_gains, a_w_in, a_q_norm, a_w_q_up, a_kv_norm, a_w_kv_up, a_w_o, b_kv_norm, b_w_kv,
           b_w_q, b_w_o, rel_bias, ffn_w_in, ffn_conv_w, ffn_conv_b, ffn_w_out):
    B, S, D = x.shape
    depth = norm_gains.shape[0]
    n_a = a_w_in.shape[0]
    L = MOBA_BLOCK
    NB = S // L
    HW = MOBA_HEADS * MOBA_HEAD
    ffn_w_in_b = ffn_w_in.astype(BF16)
    ffn_w_out_b = ffn_w_out.astype(BF16)
    ffn_conv_b3 = ffn_conv_b.reshape(depth, 1, -1)
    outs = []
    for b in range(B):
        pos = positions[b]
        pos_col = pos.reshape(S, 1)
        pos_blk = pos.reshape(NB, L)
        qmin = jnp.min(pos_blk, axis=1)
        kmax = jnp.max(pos_blk, axis=1)
        cos_t, sin_t = rope_tables(pos_col)
        tbl = (rel_bias - rel_bias[REL_BUCKETS - 1:REL_BUCKETS, :]).T * LOG2E
        tbl = jnp.pad(tbl, ((0, 0), (0, LANES - REL_BUCKETS)))
        bias_t = moba_bias_tiles(pos_blk, pos_col, tbl) if depth > n_a else None

        h = x[b]
        xn = norm_cast(h, norm_gains[0, 0])
        k = vt = kmean_p = None
        for layer in range(depth):
            g = norm_gains[layer]
            if layer < n_a:
                w1, wqn, wqr, wkn, wvt = _prep_mla(a_w_in[layer], a_w_q_up[layer], a_w_kv_up[layer])
                q_scale = (QK_NOPE + QK_ROPE) ** -0.5 * LOG2E
                qn, qr, kn, kr, vt_a = mla_proj(xn, w1, a_q_norm[layer].reshape(1, -1),
                                                a_kv_norm[layer].reshape(1, -1), wqn, wqr, wkn, wvt,
                                                cos_t, sin_t, q_scale)
                o = mla_attention(qn, qr, kn, kr, vt_a)
                w_o = a_w_o[layer]
            else:
                j = layer - n_a
                q = matmul(xn, b_w_q[j].astype(BF16), scale=MOBA_HEAD ** -0.5 * LOG2E)
                o = moba_attention(q, k, vt, kmean_p, bias_t, pos_blk, tbl, qmin, kmax)
                w_o = b_w_o[j]
            h, xn = proj_res_norm(o, w_o.astype(BF16), h, g[1], g[2])
            if layer + 1 == depth:
                g_next = None
            elif layer + 1 == n_a:
                g_next = jnp.stack([norm_gains[layer + 1, 0], b_kv_norm])
            else:
                g_next = norm_gains[layer + 1, 0].reshape(1, D)
            h, nxt = ffn(xn, layer, ffn_w_in_b, ffn_conv_w, ffn_conv_b3, ffn_w_out_b, h, g[3], g_next)
            if nxt:
                xn = nxt[0]
            if layer + 1 == n_a:
                k, kmean = matmul(nxt[1], b_w_kv[:, :HW].astype(BF16), tm=L, with_mean=True)
                vt = matmul_nt(b_w_kv[:, HW:].T.astype(BF16), nxt[1], ones_rows=True)
                kmean_p = jnp.pad(kmean.reshape(NB, HW), ((0, LANES - NB), (0, 0)))
        outs.append(h)
    return jnp.stack(outs)
```

```python
import functools
import math

import jax
import jax.numpy as jnp
from jax import lax
from jax.experimental import pallas as pl
from jax.experimental.pallas import tpu as pltpu

F32 = jnp.float32
BF16 = jnp.bfloat16

MLA_HEADS = 16
Q_LORA = 512
KV_LORA = 512
QK_NOPE = 128
QK_ROPE = 64
V_HEAD = 128
ROPE_THETA = 10000.0
MOBA_HEADS = 16
MOBA_HEAD = 128
MOBA_BLOCK = 256
MOBA_TOPK = 3
REL_BUCKETS = 32
REL_MAX_DIST = 128
CONV_WIDTH = 3
EPS = 1e-6
NEG = -1e30
LOG2E = 1.4426950408889634

LANES = 128
FAR_DIST = REL_MAX_DIST
VMEM_LIMIT = 56 * 1024 * 1024


def _params(*sem):
    return pltpu.CompilerParams(dimension_semantics=sem, vmem_limit_bytes=VMEM_LIMIT)


def _rms_scale(x):
    return lax.rsqrt(jnp.mean(x * x, axis=-1, keepdims=True) + EPS)


def _dot(a, b):
    return jnp.dot(a, b, preferred_element_type=F32)


def _dot_nt(a, b):
    return lax.dot_general(a, b, (((1,), (1,)), ((), ())), preferred_element_type=F32)


def _norm_kernel(x_ref, g_ref, o_ref):
    x = x_ref[...]
    o_ref[...] = (x * _rms_scale(x) * g_ref[...]).astype(o_ref.dtype)


def norm_cast(x, g, tm=512):
    S, D = x.shape
    tm = min(tm, S)
    return pl.pallas_call(
        _norm_kernel,
        grid=(S // tm,),
        in_specs=[pl.BlockSpec((tm, D), lambda i: (i, 0)), pl.BlockSpec((1, D), lambda i: (0, 0))],
        out_specs=pl.BlockSpec((tm, D), lambda i: (i, 0)),
        out_shape=jax.ShapeDtypeStruct((S, D), BF16),
        compiler_params=_params("parallel"),
        name="norm_cast",
    )(x, g.reshape(1, D))


def _rope_table_kernel(pos_ref, inv_ref, sgn_ref, cos_ref, sin_ref):
    ang = pos_ref[...].astype(F32) * inv_ref[...]
    cos_ref[...] = jnp.cos(ang)
    sin_ref[...] = jnp.sin(ang) * sgn_ref[...]


def rope_tables(pos_col, tm=1024):
    S = pos_col.shape[0]
    tm = min(tm, S)
    half = QK_ROPE // 2
    inv = ROPE_THETA ** (-jnp.arange(half, dtype=F32) / half)
    z = jnp.zeros((half,), F32)
    inv_pat = jnp.concatenate([inv, z, inv, z]).reshape(1, LANES)
    o = jnp.ones((2 * half,), F32)
    sgn = jnp.concatenate([-o, o]).reshape(1, LANES)
    return pl.pallas_call(
        _rope_table_kernel,
        grid=(S // tm,),
        in_specs=[pl.BlockSpec((tm, 1), lambda i: (i, 0)),
                  pl.BlockSpec((1, LANES), lambda i: (0, 0)),
                  pl.BlockSpec((1, LANES), lambda i: (0, 0))],
        out_specs=[pl.BlockSpec((tm, LANES), lambda i: (i, 0))] * 2,
        out_shape=[jax.ShapeDtypeStruct((S, LANES), F32)] * 2,
        compiler_params=_params("parallel"),
        name="rope_tables",
    )(pos_col, inv_pat, sgn)


def _mla_proj_kernel(xn_ref, w1_ref, gq_ref, gkv_ref, wqn_ref, wqr_ref, wkn_ref, wvt_ref, cos_ref, sin_ref,
                     qn_ref, qr_ref, kn_ref, kr_ref, vt_ref, *, q_scale):
    xn = xn_ref[...]
    cos = cos_ref[...]
    sin = sin_ref[...]

    def rope(x):
        return x * cos + pltpu.roll(x, LANES // 2, axis=1) * sin

    cq = _dot(xn, w1_ref[:, 0:Q_LORA])
    ckv = _dot(xn, w1_ref[:, Q_LORA:Q_LORA + KV_LORA])
    kr = _dot(xn, w1_ref[:, Q_LORA + KV_LORA:])
    kr_ref[...] = rope(kr).astype(BF16)
    cqn = (cq * _rms_scale(cq) * gq_ref[...]).astype(BF16)
    ckvn = (ckv * _rms_scale(ckv) * gkv_ref[...]).astype(BF16)
    width = MLA_HEADS * QK_NOPE
    chunk = 4 * LANES
    for c in range(width // chunk):
        sl = slice(c * chunk, (c + 1) * chunk)
        qn_ref[:, sl] = (_dot(cqn, wqn_ref[:, sl]) * q_scale).astype(BF16)
        qr = _dot(cqn, wqr_ref[:, sl])
        for hh in range(chunk // LANES):
            x = qr[:, hh * LANES:(hh + 1) * LANES]
            lo = c * chunk + hh * LANES
            qr_ref[:, lo:lo + LANES] = (rope(x) * q_scale).astype(BF16)
        kn_ref[:, sl] = _dot(ckvn, wkn_ref[:, sl]).astype(BF16)
        vt = _dot_nt(wvt_ref[sl, :], ckvn).astype(BF16)
        hpc = chunk // LANES
        vt_ref[c * hpc * VT_ROWS:(c + 1) * hpc * VT_ROWS, :] = _with_ones_rows(vt, hpc)


def mla_proj(xn, w1, gq, gkv, wqn, wqr, wkn, wvt, cos_t, sin_t, q_scale, tm=256):
    S, D = xn.shape
    tm = min(tm, S)
    W = MLA_HEADS * QK_NOPE
    full = lambda a: pl.BlockSpec(a.shape, lambda i: (0, 0))
    row = lambda n: pl.BlockSpec((tm, n), lambda i: (i, 0))
    return pl.pallas_call(
        functools.partial(_mla_proj_kernel, q_scale=q_scale),
        grid=(S // tm,),
        in_specs=[row(D), full(w1), full(gq), full(gkv), full(wqn), full(wqr), full(wkn), full(wvt),
                  row(LANES), row(LANES)],
        out_specs=[row(W), row(W), row(W), row(LANES), pl.BlockSpec((MLA_HEADS * VT_ROWS, tm), lambda i: (0, i))],
        out_shape=[jax.ShapeDtypeStruct((S, W), BF16), jax.ShapeDtypeStruct((S, W), BF16),
                   jax.ShapeDtypeStruct((S, W), BF16), jax.ShapeDtypeStruct((S, LANES), BF16),
                   jax.ShapeDtypeStruct((MLA_HEADS * VT_ROWS, S), BF16)],
        compiler_params=_params("parallel"),
        name="mla_proj",
    )(xn, w1, gq, gkv, wqn, wqr, wkn, wvt, cos_t, sin_t)


ONES_ROWS = 16
VT_ROWS = LANES + ONES_ROWS


def _flash_init(m_ref, acc_ref, g):
    m_ref[g] = jnp.full(m_ref.shape[1:], NEG, F32)
    acc_ref[g] = jnp.zeros(acc_ref.shape[1:], F32)


def _flash_softmax(st, m_ref, g, off=None):
    m_prev = m_ref[g]
    m_new = jnp.maximum(m_prev, jnp.max(st, axis=0, keepdims=True))
    alpha = jnp.exp2(m_prev - m_new)
    p = jnp.exp2(st - (m_new if off is None else m_new + off))
    m_ref[g] = m_new
    return p.astype(BF16), alpha


def _flash_accumulate(vt, p, alpha, acc_ref, g):
    acc_ref[g] = alpha * acc_ref[g] + _dot(vt, p)


def _flash_out(acc_ref, g):
    acc = acc_ref[g]
    return (acc[0:LANES] / acc[LANES:LANES + 1]).T


def _with_ones_rows(vt, heads):
    ones = jnp.ones((ONES_ROWS, vt.shape[1]), vt.dtype)
    parts = []
    for h in range(heads):
        parts += [vt[h * LANES:(h + 1) * LANES], ones]
    return jnp.concatenate(parts, axis=0)


def _flash_pipeline(n, heads, first_score, score, vt_tile, off, s_ref, p_ref, a_ref, m_ref, acc_ref):
    for g in range(heads):
        _flash_init(m_ref, acc_ref, g)
        p_ref[g, 1] = jnp.zeros(p_ref.shape[2:], BF16)
        a_ref[g, 1] = jnp.ones(a_ref.shape[2:], F32)
        s_ref[g, 0] = first_score(g)

    def tick(t, a, b):
        for g in range(heads):
            s_ref[g, b] = score(t + 1, g)
        tp = jnp.maximum(t - 1, 0)
        for g in range(heads):
            _flash_accumulate(vt_tile(tp, g), p_ref[g, b], a_ref[g, b], acc_ref, g)
        o = None if off is None else off(t)
        for g in range(heads):
            p, alpha = _flash_softmax(s_ref[g, a], m_ref, g, o)
            p_ref[g, a] = p
            a_ref[g, a] = alpha

    def body(k, carry):
        tick(2 * k, 0, 1)
        tick(2 * k + 1, 1, 0)
        return carry

    lax.fori_loop(0, n // 2, body, 0)

    @pl.when(n % 2 == 1)
    def _():
        tick(n - 1, 0, 1)

    def last(e):
        tp = jnp.maximum(n - 1, 0)
        for g in range(heads):
            _flash_accumulate(vt_tile(tp, g), p_ref[g, 1 - e], a_ref[g, 1 - e], acc_ref, g)
        o = None if off is None else off(n)
        for g in range(heads):
            p, alpha = _flash_softmax(s_ref[g, e], m_ref, g, o)
            _flash_accumulate(vt_tile(n, g), p, alpha, acc_ref, g)

    @pl.when(n % 2 == 0)
    def _():
        last(0)

    @pl.when(n % 2 == 1)
    def _():
        last(1)


def _flash_scratch(heads, tk, tq):
    return [pltpu.VMEM((heads, 2, tk, tq), F32), pltpu.VMEM((heads, 2, tk, tq), BF16),
            pltpu.VMEM((heads, 2, 1, tq), F32), pltpu.VMEM((heads, 1, tq), F32),
            pltpu.VMEM((heads, VT_ROWS, tq), F32)]


def _mla_attn_kernel(qn_ref, qr_ref, kn_ref, kr_ref, vt_ref, o_ref, q_ref, *flash_refs, tq, heads):
    i = pl.program_id(1)
    for g in range(heads):
        hs = slice(g * LANES, (g + 1) * LANES)
        q_ref[g, :, 0:LANES] = qn_ref[:, hs]
        q_ref[g, :, LANES:] = qr_ref[:, hs]

    def rows(t):
        return pl.ds(pl.multiple_of(jnp.where(t == 0, i, t - 1) * tq, tq), tq)

    def score(t, g):
        k = jnp.concatenate([kn_ref[rows(t), g * LANES:(g + 1) * LANES], kr_ref[rows(t), :]], axis=1)
        return _dot_nt(k, q_ref[g])

    def vt_tile(t, g):
        return vt_ref[g * VT_ROWS:(g + 1) * VT_ROWS, rows(t)]

    def diagonal(g):
        kidx = lax.broadcasted_iota(jnp.int32, (tq, tq), 0)
        qidx = lax.broadcasted_iota(jnp.int32, (tq, tq), 1)
        return jnp.where(kidx <= qidx, score(0, g), NEG)

    _flash_pipeline(i, heads, diagonal, score, vt_tile, None, *flash_refs)
    acc_ref = flash_refs[-1]
    for g in range(heads):
        o_ref[:, g * LANES:(g + 1) * LANES] = _flash_out(acc_ref, g).astype(o_ref.dtype)


def mla_attention(qn, qr, kn, kr, vt, tq=512, heads=2):
    S, W = qn.shape
    tq = min(tq, S)
    gw = heads * LANES
    qspec = pl.BlockSpec((tq, gw), lambda h, i: (i, h))
    return pl.pallas_call(
        functools.partial(_mla_attn_kernel, tq=tq, heads=heads),
        grid=(W // gw, S // tq),
        in_specs=[qspec, qspec,
                  pl.BlockSpec((S, gw), lambda h, i: (0, h)),
                  pl.BlockSpec((S, LANES), lambda h, i: (0, 0)),
                  pl.BlockSpec((heads * VT_ROWS, S), lambda h, i: (h, 0))],
        out_specs=qspec,
        out_shape=jax.ShapeDtypeStruct((S, W), BF16),
        scratch_shapes=[pltpu.VMEM((heads, tq, 2 * LANES), BF16)] + _flash_scratch(heads, tq, tq),
        compiler_params=_params("parallel", "arbitrary"),
        name="mla_attention",
    )(qn, qr, kn, kr, vt)


def _proj_res_kernel(o_ref, w_ref, h_ref, gpost_ref, gnext_ref, hn_ref, xn_ref, mix_ref):
    n = w_ref.shape[1]
    chunk = 4 * LANES
    th = o_ref.shape[0] // 2
    for r in range(2):
        rows = pl.ds(r * th, th)
        o = o_ref[rows, :]
        for c in range(n // chunk):
            sl = slice(c * chunk, (c + 1) * chunk)
            mix_ref[rows, sl] = _dot(o, w_ref[:, sl])
        mix = mix_ref[rows, :]
        hn = h_ref[rows, :] + mix * _rms_scale(mix) * gpost_ref[...]
        hn_ref[rows, :] = hn
        xn_ref[rows, :] = (hn * _rms_scale(hn) * gnext_ref[...]).astype(xn_ref.dtype)


def proj_res_norm(o, w, h, g_post, g_next, tm=512):
    S, K = o.shape
    D = w.shape[1]
    tm = min(tm, S)
    row = lambda n: pl.BlockSpec((tm, n), lambda i: (i, 0))
    gspec = pl.BlockSpec((1, D), lambda i: (0, 0))
    return pl.pallas_call(
        _proj_res_kernel,
        grid=(S // tm,),
        in_specs=[row(K), pl.BlockSpec((K, D), lambda i: (0, 0)), row(D), gspec, gspec],
        out_specs=[row(D), row(D)],
        out_shape=[jax.ShapeDtypeStruct((S, D), F32), jax.ShapeDtypeStruct((S, D), BF16)],
        scratch_shapes=[pltpu.VMEM((tm, D), F32)],
        compiler_params=_params("parallel"),
        name="proj_res_norm",
    )(o, w, h, g_post.reshape(1, D), g_next.reshape(1, D))


CARRY = 8


def _gelu_tanh(x):
    return 0.5 * x * (1.0 + jnp.tanh(math.sqrt(2.0 / math.pi) * (x + 0.044715 * (x * x * x))))


def _ffn_kernel(xn_ref, wg_ref, wu_ref, cwg_ref, cwu_ref, cbg_ref, cbu_ref, wo_ref, h_ref,
                gpost_ref, gnext_ref, hn_ref, *rest, n_next):
    xn_next_refs = rest[:n_next]
    yg_ref, yu_ref, cg_ref, cu_ref, acc_ref = rest[n_next:]
    i = pl.program_id(0)
    c = pl.program_id(1)
    tm = xn_ref.shape[0]
    th = tm // 2

    @pl.when(c == 0)
    def _():
        acc_ref[...] = jnp.zeros(acc_ref.shape, F32)

    for y_ref, carry_ref in ((yg_ref, cg_ref), (yu_ref, cu_ref)):
        prev = carry_ref[c]
        y_ref[0:CARRY, :] = jnp.where(i == 0, jnp.zeros_like(prev), prev)

    def conv(r, w_ref, y_ref, cw_ref, cb_ref):
        base = CARRY + r * th
        y_ref[pl.ds(base, th), :] = _dot(xn_ref[pl.ds(r * th, th), :], w_ref[...])
        cw = cw_ref[...]
        out = (y_ref[pl.ds(base - 2, th), :] * cw[0:1] + y_ref[pl.ds(base - 1, th), :] * cw[1:2]
               + y_ref[pl.ds(base, th), :] * cw[2:3])
        return out + cb_ref[...]

    acts = []
    for r in range(2):
        act = _gelu_tanh(conv(r, wg_ref, yg_ref, cwg_ref, cbg_ref)) * conv(r, wu_ref, yu_ref, cwu_ref, cbu_ref)
        acts.append(act.astype(BF16))
    cg_ref[c] = yg_ref[pl.ds(tm, CARRY), :]
    cu_ref[c] = yu_ref[pl.ds(tm, CARRY), :]
    for r in range(2):
        acc_ref[pl.ds(r * th, th), :] += _dot(acts[r], wo_ref[...])

    @pl.when(c == pl.num_programs(1) - 1)
    def _():
        f = acc_ref[...]
        hn = h_ref[...] + f * _rms_scale(f) * gpost_ref[...]
        hn_ref[...] = hn
        if n_next:
            y = hn * _rms_scale(hn)
            for k in range(n_next):
                xn_next_refs[k][...] = (y * gnext_ref[k:k + 1, :]).astype(BF16)


def ffn(xn, layer, w_in, conv_w, conv_b, w_out, h, g_post, g_next, tm=512, tf=512):
    S, D = xn.shape
    FF = w_out.shape[1]
    tm = min(tm, S)
    nf = FF // tf
    n_next = 0 if g_next is None else g_next.shape[0]
    if g_next is None:
        g_next = jnp.ones((1, D), F32)
    cb = conv_b
    row = pl.BlockSpec((tm, D), lambda i, c: (i, 0))
    in_specs = [
        row,
        pl.BlockSpec((None, D, tf), lambda i, c: (layer, 0, c)),
        pl.BlockSpec((None, D, tf), lambda i, c: (layer, 0, c + nf)),
        pl.BlockSpec((None, CONV_WIDTH, tf), lambda i, c: (layer, 0, c)),
        pl.BlockSpec((None, CONV_WIDTH, tf), lambda i, c: (layer, 0, c + nf)),
        pl.BlockSpec((None, 1, tf), lambda i, c: (layer, 0, c)),
        pl.BlockSpec((None, 1, tf), lambda i, c: (layer, 0, c + nf)),
        pl.BlockSpec((None, tf, D), lambda i, c: (layer, c, 0)),
        row,
        pl.BlockSpec((1, D), lambda i, c: (0, 0)),
        pl.BlockSpec(g_next.shape, lambda i, c: (0, 0)),
    ]
    outs = pl.pallas_call(
        functools.partial(_ffn_kernel, n_next=n_next),
        grid=(S // tm, nf),
        in_specs=in_specs,
        out_specs=[row] * (1 + n_next),
        out_shape=[jax.ShapeDtypeStruct((S, D), F32)] + [jax.ShapeDtypeStruct((S, D), BF16)] * n_next,
        scratch_shapes=[pltpu.VMEM((CARRY + tm, tf), F32), pltpu.VMEM((CARRY + tm, tf), F32),
                        pltpu.VMEM((nf, CARRY, tf), F32), pltpu.VMEM((nf, CARRY, tf), F32),
                        pltpu.VMEM((tm, D), F32)],
        compiler_params=_params("arbitrary", "arbitrary"),
        name="conv_glu_ffn",
    )(xn, w_in, w_in, conv_w, conv_w, cb, cb, w_out, h, g_post.reshape(1, D), g_next)
    return outs[0], list(outs[1:])


def _matmul_kernel(x_ref, w_ref, o_ref, *mean_ref, scale):
    y = _dot(x_ref[...], w_ref[...])
    if scale != 1.0:
        y = y * scale
    o_ref[...] = y.astype(o_ref.dtype)
    if mean_ref:
        mean_ref[0][...] = jnp.mean(y, axis=0, keepdims=True)[None]


def matmul(x, w, scale=1.0, tm=512, tn=1024, with_mean=False):
    S, K = x.shape
    N = w.shape[1]
    tm = min(tm, S)
    out_specs = [pl.BlockSpec((tm, tn), lambda n, i: (i, n))]
    out_shape = [jax.ShapeDtypeStruct((S, N), BF16)]
    if with_mean:
        out_specs.append(pl.BlockSpec((1, 1, tn), lambda n, i: (i, 0, n)))
        out_shape.append(jax.ShapeDtypeStruct((S // tm, 1, N), F32))
    outs = pl.pallas_call(
        functools.partial(_matmul_kernel, scale=scale),
        grid=(N // tn, S // tm),
        in_specs=[pl.BlockSpec((tm, K), lambda n, i: (i, 0)), pl.BlockSpec((K, tn), lambda n, i: (0, n))],
        out_specs=out_specs,
        out_shape=out_shape,
        compiler_params=_params("parallel", "parallel"),
        name="matmul_mean" if with_mean else "matmul",
    )(x, w)
    return outs if with_mean else outs[0]


def _matmul_nt_kernel(a_ref, b_ref, o_ref, *, heads):
    y = _dot_nt(a_ref[...], b_ref[...]).astype(o_ref.dtype)
    o_ref[...] = _with_ones_rows(y, heads) if heads else y


def matmul_nt(a, b, tm=1024, tn=512, ones_rows=False):
    M, K = a.shape
    N = b.shape[0]
    tm, tn = min(tm, M), min(tn, N)
    heads = tm // LANES if ones_rows else 0
    to = heads * VT_ROWS if ones_rows else tm
    return pl.pallas_call(
        functools.partial(_matmul_nt_kernel, heads=heads),
        grid=(M // tm, N // tn),
        in_specs=[pl.BlockSpec((tm, K), lambda m, n: (m, 0)), pl.BlockSpec((tn, K), lambda m, n: (n, 0))],
        out_specs=pl.BlockSpec((to, tn), lambda m, n: (m, n)),
        out_shape=jax.ShapeDtypeStruct((M // tm * to, N), BF16),
        compiler_params=_params("parallel", "parallel"),
        name="matmul_nt",
    )(a, b)


def _rel_bucket(dist):
    n = jnp.maximum(dist, 0)
    max_exact = REL_BUCKETS // 2
    nf = jnp.maximum(n, 1).astype(F32)
    large = max_exact + (jnp.log(nf / max_exact) / math.log(REL_MAX_DIST / max_exact)
                         * (REL_BUCKETS - max_exact)).astype(jnp.int32)
    large = jnp.minimum(large, REL_BUCKETS - 1)
    return jnp.where(n < max_exact, n, large)


def _bias_lookup(bucket, tbl_row):
    rows, cols = bucket.shape
    tb = jnp.broadcast_to(tbl_row, (rows, LANES))
    parts = [jnp.take_along_axis(tb, bucket[:, c * LANES:(c + 1) * LANES], axis=1) for c in range(cols // LANES)]
    return jnp.concatenate(parts, axis=1)


def _moba_bias_kernel(posq_ref, pk0_ref, pk1_ref, tbl_ref, o_ref):
    L = MOBA_BLOCK
    posq = posq_ref[0]
    for half, pk_ref in enumerate((pk0_ref, pk1_ref)):
        bucket = _rel_bucket(posq - pk_ref[...])
        for h in range(o_ref.shape[1]):
            o_ref[0, h, half * L:(half + 1) * L, :] = _bias_lookup(bucket, tbl_ref[h:h + 1, :])


def moba_bias_tiles(pos_blk, pos_col, tbl):
    NB, L = pos_blk.shape
    H = tbl.shape[0]
    near0 = lambda i: jnp.maximum(i - 1, 0)
    return pl.pallas_call(
        _moba_bias_kernel,
        grid=(NB,),
        in_specs=[pl.BlockSpec((1, 1, L), lambda i: (i, 0, 0)),
                  pl.BlockSpec((L, 1), lambda i: (near0(i), 0)),
                  pl.BlockSpec((L, 1), lambda i: (near0(i) + 1, 0)),
                  pl.BlockSpec(tbl.shape, lambda i: (0, 0))],
        out_specs=pl.BlockSpec((1, H, 2 * L, L), lambda i: (i, 0, 0, 0)),
        out_shape=jax.ShapeDtypeStruct((NB, H, 2 * L, L), F32),
        compiler_params=_params("parallel"),
        name="moba_bias_tiles",
    )(pos_blk.reshape(NB, 1, L), pos_col, pos_col, tbl)


def _moba_kernel(qmin_ref, kmax_ref, q_ref, k_ref, vt_ref, kmean_ref, bias_ref, posq_ref, posk_ref, tbl_ref,
                 o_ref, qaug_ref, qnear_ref, *flash_refs, heads, nb):
    hg = pl.program_id(0)
    i = pl.program_id(1)
    L = MOBA_BLOCK
    P = 2 * L
    n = i // 2
    b0 = jnp.maximum(i - 1, 0)
    nbp = min(LANES, -(-nb // 8) * 8)
    blk = lax.broadcasted_iota(jnp.int32, (nbp, L), 0)
    for g in range(heads):
        hs = slice(g * LANES, (g + 1) * LANES)
        q = q_ref[:, hs]
        gate = _dot_nt(kmean_ref[:, hs].astype(BF16), q)[:nbp]
        gate = jnp.where(blk < i, gate, -jnp.inf)
        keep = jnp.zeros((nbp, L), F32)
        for _ in range(MOBA_TOPK):
            mx = jnp.max(gate, axis=0, keepdims=True)
            first = jnp.min(jnp.where(gate == mx, blk, nbp), axis=0, keepdims=True)
            hit = blk == first
            keep = jnp.where(hit, 1.0, keep)
            gate = jnp.where(hit, -jnp.inf, gate)
        keep = jnp.where(blk < i, keep, jnp.where(blk == i, 1.0, 0.0))
        pen = jnp.where(keep > 0.0, 0.0, NEG)
        if nbp < LANES:
            pen = jnp.concatenate([pen, jnp.zeros((LANES - nbp, L), F32)], axis=0)
        pen_t = pen.T
        blk_t = lax.broadcasted_iota(jnp.int32, (L, LANES), 1)
        pen_sweep_t = jnp.where(blk_t >= b0, NEG, pen_t)
        qnear_ref[g, :, 0:LANES] = q
        qnear_ref[g, :, LANES:] = pen_t.astype(BF16)
        qaug_ref[g, :, 0:LANES] = q
        qaug_ref[g, :, LANES:] = pen_sweep_t.astype(BF16)

    lane = lax.broadcasted_iota(jnp.int32, (P, LANES), 1)
    half = jnp.where(lax.broadcasted_iota(jnp.int32, (P, LANES), 0) >= L, 1, 0)

    def rows(t):
        return pl.ds(pl.multiple_of(jnp.where(t == 0, b0 * L, (t - 1) * P), L), P)

    def raw_score(t, g, first_blk, qa_ref):
        onehot = jnp.where(lane == first_blk + half, 1.0, 0.0).astype(BF16)
        kaug = jnp.concatenate([k_ref[rows(t), g * LANES:(g + 1) * LANES], onehot], axis=1)
        return _dot_nt(kaug, qa_ref[g])

    def score(t, g):
        return raw_score(t, g, 2 * (t - 1), qaug_ref)

    def vt_tile(t, g):
        return vt_ref[g * VT_ROWS:(g + 1) * VT_ROWS, rows(t)]

    def first_score(g):
        kidx = b0 * L + lax.broadcasted_iota(jnp.int32, (P, L), 0)
        qidx = i * L + lax.broadcasted_iota(jnp.int32, (P, L), 1)
        return jnp.where(kidx <= qidx, raw_score(0, g, b0, qnear_ref) + bias_ref[0, g], NEG)

    def near(t):
        def blk_near(b):
            return jnp.logical_and(b < b0, qmin_ref[i] - kmax_ref[b] < FAR_DIST)
        return jnp.logical_or(blk_near(2 * (t - 1)), blk_near(2 * (t - 1) + 1))

    _flash_pipeline(n, heads, first_score, score, vt_tile,
                    lambda t: jnp.where(jnp.logical_and(t > 0, near(jnp.maximum(t, 1))), -NEG, 0.0), *flash_refs)
    m_ref, acc_ref = flash_refs[-2:]

    def redo(t, carry):
        @pl.when(near(t))
        def _():
            posk = jnp.broadcast_to(posk_ref[t - 1], (LANES, P)).T[:, 0:1]
            bucket = _rel_bucket(posq_ref[0] - posk)
            for g in range(heads):
                s = score(t, g) + _bias_lookup(bucket, tbl_ref[pl.ds(hg * heads + g, 1), :])
                p, alpha = _flash_softmax(s, m_ref, g)
                _flash_accumulate(vt_tile(t, g), p, alpha, acc_ref, g)
        return carry

    lax.fori_loop(1, n + 1, redo, 0)
    for g in range(heads):
        o_ref[:, g * LANES:(g + 1) * LANES] = _flash_out(acc_ref, g).astype(o_ref.dtype)


def moba_attention(q, k, vt, kmean_p, bias_t, pos_blk, tbl, qmin, kmax, heads=4):
    S, W = q.shape
    L = MOBA_BLOCK
    NB = S // L
    assert NB % 2 == 0
    gw = heads * LANES
    qspec = pl.BlockSpec((L, gw), lambda h, i, *_: (i, h))
    pos_pair = pos_blk.reshape(NB // 2, 1, 2 * L)
    grid_spec = pltpu.PrefetchScalarGridSpec(
        num_scalar_prefetch=2,
        grid=(W // gw, NB),
        in_specs=[
            qspec,
            pl.BlockSpec((S, gw), lambda h, i, *_: (0, h)),
            pl.BlockSpec((heads * VT_ROWS, S), lambda h, i, *_: (h, 0)),
            pl.BlockSpec((kmean_p.shape[0], gw), lambda h, i, *_: (0, h)),
            pl.BlockSpec((1, heads, 2 * L, L), lambda h, i, *_: (i, h, 0, 0)),
            pl.BlockSpec((1, 1, L), lambda h, i, *_: (i, 0, 0)),
            pl.BlockSpec(pos_pair.shape, lambda h, i, *_: (0, 0, 0)),
            pl.BlockSpec(tbl.shape, lambda h, i, *_: (0, 0)),
        ],
        out_specs=qspec,
        scratch_shapes=[pltpu.VMEM((heads, L, 2 * LANES), BF16)] * 2 + _flash_scratch(heads, 2 * L, L),
    )
    return pl.pallas_call(
        functools.partial(_moba_kernel, heads=heads, nb=NB),
        grid_spec=grid_spec,
        out_shape=jax.ShapeDtypeStruct((S, W), BF16),
        compiler_params=_params("parallel", "arbitrary"),
        name="moba_attention",
    )(qmin, kmax, q, k, vt, kmean_p, bias_t, pos_blk.reshape(NB, 1, L), pos_pair, tbl)


def _rope_lanes(w):
    half = QK_ROPE // 2
    z = jnp.zeros(w.shape[:-1] + (half,), w.dtype)
    return jnp.concatenate([w[..., :half], z, w[..., half:], z], axis=-1)


def _prep_mla(w_in, w_q_up, w_kv_up):
    w1 = jnp.concatenate([w_in[:, :Q_LORA + KV_LORA], _rope_lanes(w_in[:, Q_LORA + KV_LORA:])], axis=1)
    wq = w_q_up.reshape(Q_LORA, MLA_HEADS, QK_NOPE + QK_ROPE)
    wqn = wq[:, :, :QK_NOPE].reshape(Q_LORA, MLA_HEADS * QK_NOPE)
    wqr = _rope_lanes(wq[:, :, QK_NOPE:]).reshape(Q_LORA, MLA_HEADS * LANES)
    wkv = w_kv_up.reshape(KV_LORA, MLA_HEADS, QK_NOPE + V_HEAD)
    wkn = wkv[:, :, :QK_NOPE].reshape(KV_LORA, -1)
    wvt = wkv[:, :, QK_NOPE:].reshape(KV_LORA, -1).T
    return [w.astype(BF16) for w in (w1, wqn, wqr, wkn, wvt)]


def kernel(x, positions, norm_gains, a_w_in, a_q_norm, a_w_q_up, a_kv_norm, a_w_kv_up, a_w_o, b_kv_norm, b_w_kv,
           b_w_q, b_w_o, rel_bias, ffn_w_in, ffn_conv_w, ffn_conv_b, ffn_w_out):
    B, S, D = x.shape
    depth = norm_gains.shape[0]
    n_a = a_w_in.shape[0]
    L = MOBA_BLOCK
    NB = S // L
    HW = MOBA_HEADS * MOBA_HEAD
    ffn_w_in_b = ffn_w_in.astype(BF16)
    ffn_w_out_b = ffn_w_out.astype(BF16)
    ffn_conv_b3 = ffn_conv_b.reshape(depth, 1, -1)
    outs = []
    for b in range(B):
        pos = positions[b]
        pos_col = pos.reshape(S, 1)
        pos_blk = pos.reshape(NB, L)
        qmin = jnp.min(pos_blk, axis=1)
        kmax = jnp.max(pos_blk, axis=1)
        cos_t, sin_t = rope_tables(pos_col)
        tbl = (rel_bias - rel_bias[REL_BUCKETS - 1:REL_BUCKETS, :]).T * LOG2E
        tbl = jnp.pad(tbl, ((0, 0), (0, LANES - REL_BUCKETS)))
        bias_t = moba_bias_tiles(pos_blk, pos_col, tbl) if depth > n_a else None

        h = x[b]
        xn = norm_cast(h, norm_gains[0, 0])
        k = vt = kmean_p = None
        for layer in range(depth):
            g = norm_gains[layer]
            if layer < n_a:
                w1, wqn, wqr, wkn, wvt = _prep_mla(a_w_in[layer], a_w_q_up[layer], a_w_kv_up[layer])
                q_scale = (QK_NOPE + QK_ROPE) ** -0.5 * LOG2E
                qn, qr, kn, kr, vt_a = mla_proj(xn, w1, a_q_norm[layer].reshape(1, -1),
                                                a_kv_norm[layer].reshape(1, -1), wqn, wqr, wkn, wvt,
                                                cos_t, sin_t, q_scale)
                o = mla_attention(qn, qr, kn, kr, vt_a)
                w_o = a_w_o[layer]
            else:
                j = layer - n_a
                q = matmul(xn, b_w_q[j].astype(BF16), scale=MOBA_HEAD ** -0.5 * LOG2E)
                o = moba_attention(q, k, vt, kmean_p, bias_t, pos_blk, tbl, qmin, kmax)
                w_o = b_w_o[j]
            h, xn = proj_res_norm(o, w_o.astype(BF16), h, g[1], g[2])
            if layer + 1 == depth:
                g_next = None
            elif layer + 1 == n_a:
                g_next = jnp.stack([norm_gains[layer + 1, 0], b_kv_norm])
            else:
                g_next = norm_gains[layer + 1, 0].reshape(1, D)
            h, nxt = ffn(xn, layer, ffn_w_in_b, ffn_conv_w, ffn_conv_b3, ffn_w_out_b, h, g[3], g_next)
            if nxt:
                xn = nxt[0]
            if layer + 1 == n_a:
                k, kmean = matmul(nxt[1], b_w_kv[:, :HW].astype(BF16), tm=L, with_mean=True)
                vt = matmul_nt(b_w_kv[:, HW:].T.astype(BF16), nxt[1], ones_rows=True)
                kmean_p = jnp.pad(kmean.reshape(NB, HW), ((0, LANES - NB), (0, 0)))
        outs.append(h)
    return jnp.stack(outs)
```

```python
import functools
import math

import jax
import jax.numpy as jnp
from jax import lax
from jax.experimental import pallas as pl
from jax.experimental.pallas import tpu as pltpu

F32 = jnp.float32
BF16 = jnp.bfloat16

MLA_HEADS = 16
Q_LORA = 512
KV_LORA = 512
QK_NOPE = 128
QK_ROPE = 64
V_HEAD = 128
ROPE_THETA = 10000.0
MOBA_HEADS = 16
MOBA_HEAD = 128
MOBA_BLOCK = 256
MOBA_TOPK = 3
REL_BUCKETS = 32
REL_MAX_DIST = 128
CONV_WIDTH = 3
EPS = 1e-6
NEG = -1e30
LOG2E = 1.4426950408889634

LANES = 128
FAR_DIST = REL_MAX_DIST
VMEM_LIMIT = 56 * 1024 * 1024


def _params(*sem):
    return pltpu.CompilerParams(dimension_semantics=sem, vmem_limit_bytes=VMEM_LIMIT)


def _rms_scale(x):
    return lax.rsqrt(jnp.mean(x * x, axis=-1, keepdims=True) + EPS)


def _dot(a, b):
    return jnp.dot(a, b, preferred_element_type=F32)


def _dot_nt(a, b):
    return lax.dot_general(a, b, (((1,), (1,)), ((), ())), preferred_element_type=F32)


def _norm_kernel(x_ref, g_ref, o_ref):
    x = x_ref[...]
    o_ref[...] = (x * _rms_scale(x) * g_ref[...]).astype(o_ref.dtype)


def norm_cast(x, g, tm=512):
    S, D = x.shape
    tm = min(tm, S)
    return pl.pallas_call(
        _norm_kernel,
        grid=(S // tm,),
        in_specs=[pl.BlockSpec((tm, D), lambda i: (i, 0)), pl.BlockSpec((1, D), lambda i: (0, 0))],
        out_specs=pl.BlockSpec((tm, D), lambda i: (i, 0)),
        out_shape=jax.ShapeDtypeStruct((S, D), BF16),
        compiler_params=_params("parallel"),
        name="norm_cast",
    )(x, g.reshape(1, D))


def _rope_table_kernel(pos_ref, inv_ref, sgn_ref, cos_ref, sin_ref):
    ang = pos_ref[...].astype(F32) * inv_ref[...]
    cos_ref[...] = jnp.cos(ang)
    sin_ref[...] = jnp.sin(ang) * sgn_ref[...]


def rope_tables(pos_col, tm=1024):
    S = pos_col.shape[0]
    tm = min(tm, S)
    half = QK_ROPE // 2
    inv = ROPE_THETA ** (-jnp.arange(half, dtype=F32) / half)
    z = jnp.zeros((half,), F32)
    inv_pat = jnp.concatenate([inv, z, inv, z]).reshape(1, LANES)
    o = jnp.ones((2 * half,), F32)
    sgn = jnp.concatenate([-o, o]).reshape(1, LANES)
    return pl.pallas_call(
        _rope_table_kernel,
        grid=(S // tm,),
        in_specs=[pl.BlockSpec((tm, 1), lambda i: (i, 0)),
                  pl.BlockSpec((1, LANES), lambda i: (0, 0)),
                  pl.BlockSpec((1, LANES), lambda i: (0, 0))],
        out_specs=[pl.BlockSpec((tm, LANES), lambda i: (i, 0))] * 2,
        out_shape=[jax.ShapeDtypeStruct((S, LANES), F32)] * 2,
        compiler_params=_params("parallel"),
        name="rope_tables",
    )(pos_col, inv_pat, sgn)


def _mla_proj_kernel(xn_ref, w1_ref, gq_ref, gkv_ref, wqn_ref, wqr_ref, wkn_ref, wvt_ref, cos_ref, sin_ref,
                     qn_ref, qr_ref, kn_ref, kr_ref, vt_ref, *, q_scale):
    xn = xn_ref[...]
    cos = cos_ref[...]
    sin = sin_ref[...]

    def rope(x):
        return x * cos + pltpu.roll(x, LANES // 2, axis=1) * sin

    cq = _dot(xn, w1_ref[:, 0:Q_LORA])
    ckv = _dot(xn, w1_ref[:, Q_LORA:Q_LORA + KV_LORA])
    kr = _dot(xn, w1_ref[:, Q_LORA + KV_LORA:])
    kr_ref[...] = rope(kr).astype(BF16)
    cqn = (cq * _rms_scale(cq) * gq_ref[...]).astype(BF16)
    ckvn = (ckv * _rms_scale(ckv) * gkv_ref[...]).astype(BF16)
    width = MLA_HEADS * QK_NOPE
    chunk = 4 * LANES
    for c in range(width // chunk):
        sl = slice(c * chunk, (c + 1) * chunk)
        qn_ref[:, sl] = (_dot(cqn, wqn_ref[:, sl]) * q_scale).astype(BF16)
        qr = _dot(cqn, wqr_ref[:, sl])
        for hh in range(chunk // LANES):
            x = qr[:, hh * LANES:(hh + 1) * LANES]
            lo = c * chunk + hh * LANES
            qr_ref[:, lo:lo + LANES] = (rope(x) * q_scale).astype(BF16)
        kn_ref[:, sl] = _dot(ckvn, wkn_ref[:, sl]).astype(BF16)
        vt = _dot_nt(wvt_ref[sl, :], ckvn).astype(BF16)
        hpc = chunk // LANES
        vt_ref[c * hpc * VT_ROWS:(c + 1) * hpc * VT_ROWS, :] = _with_ones_rows(vt, hpc)


def mla_proj(xn, w1, gq, gkv, wqn, wqr, wkn, wvt, cos_t, sin_t, q_scale, tm=256):
    S, D = xn.shape
    tm = min(tm, S)
    W = MLA_HEADS * QK_NOPE
    full = lambda a: pl.BlockSpec(a.shape, lambda i: (0, 0))
    row = lambda n: pl.BlockSpec((tm, n), lambda i: (i, 0))
    return pl.pallas_call(
        functools.partial(_mla_proj_kernel, q_scale=q_scale),
        grid=(S // tm,),
        in_specs=[row(D), full(w1), full(gq), full(gkv), full(wqn), full(wqr), full(wkn), full(wvt),
                  row(LANES), row(LANES)],
        out_specs=[row(W), row(W), row(W), row(LANES), pl.BlockSpec((MLA_HEADS * VT_ROWS, tm), lambda i: (0, i))],
        out_shape=[jax.ShapeDtypeStruct((S, W), BF16), jax.ShapeDtypeStruct((S, W), BF16),
                   jax.ShapeDtypeStruct((S, W), BF16), jax.ShapeDtypeStruct((S, LANES), BF16),
                   jax.ShapeDtypeStruct((MLA_HEADS * VT_ROWS, S), BF16)],
        compiler_params=_params("parallel"),
        name="mla_proj",
    )(xn, w1, gq, gkv, wqn, wqr, wkn, wvt, cos_t, sin_t)


ONES_ROWS = 16
VT_ROWS = LANES + ONES_ROWS


def _flash_init(m_ref, acc_ref, g):
    m_ref[g] = jnp.full(m_ref.shape[1:], NEG, F32)
    acc_ref[g] = jnp.zeros(acc_ref.shape[1:], F32)


def _flash_softmax(st, m_ref, g, off=None):
    m_prev = m_ref[g]
    m_new = jnp.maximum(m_prev, jnp.max(st, axis=0, keepdims=True))
    alpha = jnp.exp2(m_prev - m_new)
    p = jnp.exp2(st - (m_new if off is None else m_new + off))
    m_ref[g] = m_new
    return p.astype(BF16), alpha


def _flash_accumulate(vt, p, alpha, acc_ref, g):
    acc_ref[g] = alpha * acc_ref[g] + _dot(vt, p)


def _flash_out(acc_ref, g):
    acc = acc_ref[g]
    return (acc[0:LANES] / acc[LANES:LANES + 1]).T


def _with_ones_rows(vt, heads):
    ones = jnp.ones((ONES_ROWS, vt.shape[1]), vt.dtype)
    parts = []
    for h in range(heads):
        parts += [vt[h * LANES:(h + 1) * LANES], ones]
    return jnp.concatenate(parts, axis=0)


def _flash_pipeline(n, heads, first_score, score, vt_tile, off, s_ref, p_ref, a_ref, m_ref, acc_ref):
    for g in range(heads):
        _flash_init(m_ref, acc_ref, g)
        p_ref[g, 1] = jnp.zeros(p_ref.shape[2:], BF16)
        a_ref[g, 1] = jnp.ones(a_ref.shape[2:], F32)
        s_ref[g, 0] = first_score(g)

    def tick(t, a, b):
        for g in range(heads):
            s_ref[g, b] = score(t + 1, g)
        tp = jnp.maximum(t - 1, 0)
        for g in range(heads):
            _flash_accumulate(vt_tile(tp, g), p_ref[g, b], a_ref[g, b], acc_ref, g)
        o = None if off is None else off(t)
        for g in range(heads):
            p, alpha = _flash_softmax(s_ref[g, a], m_ref, g, o)
            p_ref[g, a] = p
            a_ref[g, a] = alpha

    def body(k, carry):
        tick(2 * k, 0, 1)
        tick(2 * k + 1, 1, 0)
        return carry

    lax.fori_loop(0, n // 2, body, 0)

    @pl.when(n % 2 == 1)
    def _():
        tick(n - 1, 0, 1)

    def last(e):
        tp = jnp.maximum(n - 1, 0)
        for g in range(heads):
            _flash_accumulate(vt_tile(tp, g), p_ref[g, 1 - e], a_ref[g, 1 - e], acc_ref, g)
        o = None if off is None else off(n)
        for g in range(heads):
            p, alpha = _flash_softmax(s_ref[g, e], m_ref, g, o)
            _flash_accumulate(vt_tile(n, g), p, alpha, acc_ref, g)

    @pl.when(n % 2 == 0)
    def _():
        last(0)

    @pl.when(n % 2 == 1)
    def _():
        last(1)


def _flash_scratch(heads, tk, tq):
    return [pltpu.VMEM((heads, 2, tk, tq), F32), pltpu.VMEM((heads, 2, tk, tq), BF16),
            pltpu.VMEM((heads, 2, 1, tq), F32), pltpu.VMEM((heads, 1, tq), F32),
            pltpu.VMEM((heads, VT_ROWS, tq), F32)]


def _mla_attn_kernel(qn_ref, qr_ref, kn_ref, kr_ref, vt_ref, o_ref, q_ref, *flash_refs, tq, heads):
    i = pl.program_id(1)
    for g in range(heads):
        hs = slice(g * LANES, (g + 1) * LANES)
        q_ref[g, :, 0:LANES] = qn_ref[:, hs]
        q_ref[g, :, LANES:] = qr_ref[:, hs]

    def rows(t):
        return pl.ds(pl.multiple_of(jnp.where(t == 0, i, t - 1) * tq, tq), tq)

    def score(t, g):
        k = jnp.concatenate([kn_ref[rows(t), g * LANES:(g + 1) * LANES], kr_ref[rows(t), :]], axis=1)
        return _dot_nt(k, q_ref[g])

    def vt_tile(t, g):
        return vt_ref[g * VT_ROWS:(g + 1) * VT_ROWS, rows(t)]

    def diagonal(g):
        kidx = lax.broadcasted_iota(jnp.int32, (tq, tq), 0)
        qidx = lax.broadcasted_iota(jnp.int32, (tq, tq), 1)
        return jnp.where(kidx <= qidx, score(0, g), NEG)

    _flash_pipeline(i, heads, diagonal, score, vt_tile, None, *flash_refs)
    acc_ref = flash_refs[-1]
    for g in range(heads):
        o_ref[:, g * LANES:(g + 1) * LANES] = _flash_out(acc_ref, g).astype(o_ref.dtype)


def mla_attention(qn, qr, kn, kr, vt, tq=512, heads=2):
    S, W = qn.shape
    tq = min(tq, S)
    gw = heads * LANES
    qspec = pl.BlockSpec((tq, gw), lambda h, i: (i, h))
    return pl.pallas_call(
        functools.partial(_mla_attn_kernel, tq=tq, heads=heads),
        grid=(W // gw, S // tq),
        in_specs=[qspec, qspec,
                  pl.BlockSpec((S, gw), lambda h, i: (0, h)),
                  pl.BlockSpec((S, LANES), lambda h, i: (0, 0)),
                  pl.BlockSpec((heads * VT_ROWS, S), lambda h, i: (h, 0))],
        out_specs=qspec,
        out_shape=jax.ShapeDtypeStruct((S, W), BF16),
        scratch_shapes=[pltpu.VMEM((heads, tq, 2 * LANES), BF16)] + _flash_scratch(heads, tq, tq),
        compiler_params=_params("parallel", "arbitrary"),
        name="mla_attention",
    )(qn, qr, kn, kr, vt)


def _proj_res_kernel(o_ref, w_ref, h_ref, gpost_ref, gnext_ref, hn_ref, xn_ref, mix_ref):
    n = w_ref.shape[1]
    chunk = 4 * LANES
    th = o_ref.shape[0] // 2
    for r in range(2):
        rows = pl.ds(r * th, th)
        o = o_ref[rows, :]
        for c in range(n // chunk):
            sl = slice(c * chunk, (c + 1) * chunk)
            mix_ref[rows, sl] = _dot(o, w_ref[:, sl])
        mix = mix_ref[rows, :]
        hn = h_ref[rows, :] + mix * _rms_scale(mix) * gpost_ref[...]
        hn_ref[rows, :] = hn
        xn_ref[rows, :] = (hn * _rms_scale(hn) * gnext_ref[...]).astype(xn_ref.dtype)


def proj_res_norm(o, w, h, g_post, g_next, tm=512):
    S, K = o.shape
    D = w.shape[1]
    tm = min(tm, S)
    row = lambda n: pl.BlockSpec((tm, n), lambda i: (i, 0))
    gspec = pl.BlockSpec((1, D), lambda i: (0, 0))
    return pl.pallas_call(
        _proj_res_kernel,
        grid=(S // tm,),
        in_specs=[row(K), pl.BlockSpec((K, D), lambda i: (0, 0)), row(D), gspec, gspec],
        out_specs=[row(D), row(D)],
        out_shape=[jax.ShapeDtypeStruct((S, D), F32), jax.ShapeDtypeStruct((S, D), BF16)],
        scratch_shapes=[pltpu.VMEM((tm, D), F32)],
        compiler_params=_params("parallel"),
        name="proj_res_norm",
    )(o, w, h, g_post.reshape(1, D), g_next.reshape(1, D))


CARRY = 8


def _gelu_tanh(x):
    return 0.5 * x * (1.0 + jnp.tanh(math.sqrt(2.0 / math.pi) * (x + 0.044715 * (x * x * x))))


def _ffn_kernel(xn_ref, wg_ref, wu_ref, cwg_ref, cwu_ref, cbg_ref, cbu_ref, wo_ref, h_ref,
                gpost_ref, gnext_ref, hn_ref, *rest, n_next):
    xn_next_refs = rest[:n_next]
    yg_ref, yu_ref, cg_ref, cu_ref, acc_ref = rest[n_next:]
    i = pl.program_id(0)
    c = pl.program_id(1)
    tm = xn_ref.shape[0]
    th = tm // 2

    @pl.when(c == 0)
    def _():
        acc_ref[...] = jnp.zeros(acc_ref.shape, F32)

    for y_ref, carry_ref in ((yg_ref, cg_ref), (yu_ref, cu_ref)):
        prev = carry_ref[c]
        y_ref[0:CARRY, :] = jnp.where(i == 0, jnp.zeros_like(prev), prev)

    def conv(r, w_ref, y_ref, cw_ref, cb_ref):
        base = CARRY + r * th
        y_ref[pl.ds(base, th), :] = _dot(xn_ref[pl.ds(r * th, th), :], w_ref[...])
        cw = cw_ref[...]
        out = (y_ref[pl.ds(base - 2, th), :] * cw[0:1] + y_ref[pl.ds(base - 1, th), :] * cw[1:2]
               + y_ref[pl.ds(base, th), :] * cw[2:3])
        return out + cb_ref[...]

    acts = []
    for r in range(2):
        act = _gelu_tanh(conv(r, wg_ref, yg_ref, cwg_ref, cbg_ref)) * conv(r, wu_ref, yu_ref, cwu_ref, cbu_ref)
        acts.append(act.astype(BF16))
    cg_ref[c] = yg_ref[pl.ds(tm, CARRY), :]
    cu_ref[c] = yu_ref[pl.ds(tm, CARRY), :]
    for r in range(2):
        acc_ref[pl.ds(r * th, th), :] += _dot(acts[r], wo_ref[...])

    @pl.when(c == pl.num_programs(1) - 1)
    def _():
        f = acc_ref[...]
        hn = h_ref[...] + f * _rms_scale(f) * gpost_ref[...]
        hn_ref[...] = hn
        if n_next:
            y = hn * _rms_scale(hn)
            for k in range(n_next):
                xn_next_refs[k][...] = (y * gnext_ref[k:k + 1, :]).astype(BF16)


def ffn(xn, layer, w_in, conv_w, conv_b, w_out, h, g_post, g_next, tm=512, tf=512):
    S, D = xn.shape
    FF = w_out.shape[1]
    tm = min(tm, S)
    nf = FF // tf
    n_next = 0 if g_next is None else g_next.shape[0]
    if g_next is None:
        g_next = jnp.ones((1, D), F32)
    cb = conv_b
    row = pl.BlockSpec((tm, D), lambda i, c: (i, 0))
    in_specs = [
        row,
        pl.BlockSpec((None, D, tf), lambda i, c: (layer, 0, c)),
        pl.BlockSpec((None, D, tf), lambda i, c: (layer, 0, c + nf)),
        pl.BlockSpec((None, CONV_WIDTH, tf), lambda i, c: (layer, 0, c)),
        pl.BlockSpec((None, CONV_WIDTH, tf), lambda i, c: (layer, 0, c + nf)),
        pl.BlockSpec((None, 1, tf), lambda i, c: (layer, 0, c)),
        pl.BlockSpec((None, 1, tf), lambda i, c: (layer, 0, c + nf)),
        pl.BlockSpec((None, tf, D), lambda i, c: (layer, c, 0)),
        row,
        pl.BlockSpec((1, D), lambda i, c: (0, 0)),
        pl.BlockSpec(g_next.shape, lambda i, c: (0, 0)),
    ]
    outs = pl.pallas_call(
        functools.partial(_ffn_kernel, n_next=n_next),
        grid=(S // tm, nf),
        in_specs=in_specs,
        out_specs=[row] * (1 + n_next),
        out_shape=[jax.ShapeDtypeStruct((S, D), F32)] + [jax.ShapeDtypeStruct((S, D), BF16)] * n_next,
        scratch_shapes=[pltpu.VMEM((CARRY + tm, tf), F32), pltpu.VMEM((CARRY + tm, tf), F32),
                        pltpu.VMEM((nf, CARRY, tf), F32), pltpu.VMEM((nf, CARRY, tf), F32),
                        pltpu.VMEM((tm, D), F32)],
        compiler_params=_params("arbitrary", "arbitrary"),
        name="conv_glu_ffn",
    )(xn, w_in, w_in, conv_w, conv_w, cb, cb, w_out, h, g_post.reshape(1, D), g_next)
    return outs[0], list(outs[1:])


def _matmul_kernel(x_ref, w_ref, o_ref, *mean_ref, scale):
    y = _dot(x_ref[...], w_ref[...])
    if scale != 1.0:
        y = y * scale
    o_ref[...] = y.astype(o_ref.dtype)
    if mean_ref:
        mean_ref[0][...] = jnp.mean(y, axis=0, keepdims=True)[None]


def matmul(x, w, scale=1.0, tm=512, tn=1024, with_mean=False):
    S, K = x.shape
    N = w.shape[1]
    tm = min(tm, S)
    out_specs = [pl.BlockSpec((tm, tn), lambda n, i: (i, n))]
    out_shape = [jax.ShapeDtypeStruct((S, N), BF16)]
    if with_mean:
        out_specs.append(pl.BlockSpec((1, 1, tn), lambda n, i: (i, 0, n)))
        out_shape.append(jax.ShapeDtypeStruct((S // tm, 1, N), F32))
    outs = pl.pallas_call(
        functools.partial(_matmul_kernel, scale=scale),
        grid=(N // tn, S // tm),
        in_specs=[pl.BlockSpec((tm, K), lambda n, i: (i, 0)), pl.BlockSpec((K, tn), lambda n, i: (0, n))],
        out_specs=out_specs,
        out_shape=out_shape,
        compiler_params=_params("parallel", "parallel"),
        name="matmul_mean" if with_mean else "matmul",
    )(x, w)
    return outs if with_mean else outs[0]


def _matmul_nt_kernel(a_ref, b_ref, o_ref, *, heads):
    y = _dot_nt(a_ref[...], b_ref[...]).astype(o_ref.dtype)
    o_ref[...] = _with_ones_rows(y, heads) if heads else y


def matmul_nt(a, b, tm=1024, tn=512, ones_rows=False):
    M, K = a.shape
    N = b.shape[0]
    tm, tn = min(tm, M), min(tn, N)
    heads = tm // LANES if ones_rows else 0
    to = heads * VT_ROWS if ones_rows else tm
    return pl.pallas_call(
        functools.partial(_matmul_nt_kernel, heads=heads),
        grid=(M // tm, N // tn),
        in_specs=[pl.BlockSpec((tm, K), lambda m, n: (m, 0)), pl.BlockSpec((tn, K), lambda m, n: (n, 0))],
        out_specs=pl.BlockSpec((to, tn), lambda m, n: (m, n)),
        out_shape=jax.ShapeDtypeStruct((M // tm * to, N), BF16),
        compiler_params=_params("parallel", "parallel"),
        name="matmul_nt",
    )(a, b)


def _rel_bucket(dist):
    n = jnp.maximum(dist, 0)
    max_exact = REL_BUCKETS // 2
    nf = jnp.maximum(n, 1).astype(F32)
    large = max_exact + (jnp.log(nf / max_exact) / math.log(REL_MAX_DIST / max_exact)
                         * (REL_BUCKETS - max_exact)).astype(jnp.int32)
    large = jnp.minimum(large, REL_BUCKETS - 1)
    return jnp.where(n < max_exact, n, large)


def _bias_lookup(bucket, tbl_row):
    rows, cols = bucket.shape
    tb = jnp.broadcast_to(tbl_row, (rows, LANES))
    parts = [jnp.take_along_axis(tb, bucket[:, c * LANES:(c + 1) * LANES], axis=1) for c in range(cols // LANES)]
    return jnp.concatenate(parts, axis=1)


def _moba_bias_kernel(posq_ref, pk0_ref, pk1_ref, tbl_ref, o_ref):
    L = MOBA_BLOCK
    posq = posq_ref[0]
    bucket = _rel_bucket(posq - pk0_ref[...])
    for h in range(o_ref.shape[1]):
        o_ref[0, h, 0:L, :] = _bias_lookup(bucket, tbl_ref[h:h + 1, :])
    bucket = _rel_bucket(posq - pk1_ref[...])
    for h in range(o_ref.shape[1]):
        for j in range(L // LANES):
            live = (j + 1) * LANES
            cols = slice(j * LANES, (j + 1) * LANES)
            o_ref[0, h, L:L + live, cols] = _bias_lookup(bucket[0:live, cols], tbl_ref[h:h + 1, :])
            if live < L:
                o_ref[0, h, L + live:2 * L, cols] = jnp.zeros((L - live, LANES), F32)


def moba_bias_tiles(pos_blk, pos_col, tbl):
    NB, L = pos_blk.shape
    H = tbl.shape[0]
    near0 = lambda i: jnp.maximum(i - 1, 0)
    return pl.pallas_call(
        _moba_bias_kernel,
        grid=(NB,),
        in_specs=[pl.BlockSpec((1, 1, L), lambda i: (i, 0, 0)),
                  pl.BlockSpec((L, 1), lambda i: (near0(i), 0)),
                  pl.BlockSpec((L, 1), lambda i: (near0(i) + 1, 0)),
                  pl.BlockSpec(tbl.shape, lambda i: (0, 0))],
        out_specs=pl.BlockSpec((1, H, 2 * L, L), lambda i: (i, 0, 0, 0)),
        out_shape=jax.ShapeDtypeStruct((NB, H, 2 * L, L), F32),
        compiler_params=_params("parallel"),
        name="moba_bias_tiles",
    )(pos_blk.reshape(NB, 1, L), pos_col, pos_col, tbl)


def _moba_kernel(qmin_ref, kmax_ref, q_ref, k_ref, vt_ref, kmean_ref, bias_ref, posq_ref, posk_ref, tbl_ref,
                 o_ref, qaug_ref, qnear_ref, *flash_refs, heads, nb):
    hg = pl.program_id(0)
    i = pl.program_id(1)
    L = MOBA_BLOCK
    P = 2 * L
    n = i // 2
    b0 = jnp.maximum(i - 1, 0)
    nbp = min(LANES, -(-nb // 8) * 8)
    blk = lax.broadcasted_iota(jnp.int32, (nbp, L), 0)
    for g in range(heads):
        hs = slice(g * LANES, (g + 1) * LANES)
        q = q_ref[:, hs]
        gate = _dot_nt(kmean_ref[:, hs].astype(BF16), q)[:nbp]
        gate = jnp.where(blk < i, gate, -jnp.inf)
        keep = jnp.zeros((nbp, L), F32)
        for _ in range(MOBA_TOPK):
            mx = jnp.max(gate, axis=0, keepdims=True)
            first = jnp.min(jnp.where(gate == mx, blk, nbp), axis=0, keepdims=True)
            hit = blk == first
            keep = jnp.where(hit, 1.0, keep)
            gate = jnp.where(hit, -jnp.inf, gate)
        keep = jnp.where(blk < i, keep, jnp.where(blk == i, 1.0, 0.0))
        pen = jnp.where(keep > 0.0, 0.0, NEG)
        if nbp < LANES:
            pen = jnp.concatenate([pen, jnp.zeros((LANES - nbp, L), F32)], axis=0)
        pen_t = pen.T
        blk_t = lax.broadcasted_iota(jnp.int32, (L, LANES), 1)
        pen_sweep_t = jnp.where(blk_t >= b0, NEG, pen_t)
        qnear_ref[g, :, 0:LANES] = q
        qnear_ref[g, :, LANES:] = pen_t.astype(BF16)
        qaug_ref[g, :, 0:LANES] = q
        qaug_ref[g, :, LANES:] = pen_sweep_t.astype(BF16)

    lane = lax.broadcasted_iota(jnp.int32, (P, LANES), 1)
    half = jnp.where(lax.broadcasted_iota(jnp.int32, (P, LANES), 0) >= L, 1, 0)

    def rows(t):
        return pl.ds(pl.multiple_of(jnp.where(t == 0, b0 * L, (t - 1) * P), L), P)

    def raw_score(t, g, first_blk, qa_ref):
        onehot = jnp.where(lane == first_blk + half, 1.0, 0.0).astype(BF16)
        kaug = jnp.concatenate([k_ref[rows(t), g * LANES:(g + 1) * LANES], onehot], axis=1)
        return _dot_nt(kaug, qa_ref[g])

    def score(t, g):
        return raw_score(t, g, 2 * (t - 1), qaug_ref)

    def vt_tile(t, g):
        return vt_ref[g * VT_ROWS:(g + 1) * VT_ROWS, rows(t)]

    def first_score(g):
        kidx = b0 * L + lax.broadcasted_iota(jnp.int32, (P, L), 0)
        qidx = i * L + lax.broadcasted_iota(jnp.int32, (P, L), 1)
        return jnp.where(kidx <= qidx, raw_score(0, g, b0, qnear_ref) + bias_ref[0, g], NEG)

    def near(t):
        def blk_near(b):
            return jnp.logical_and(b < b0, qmin_ref[i] - kmax_ref[b] < FAR_DIST)
        return jnp.logical_or(blk_near(2 * (t - 1)), blk_near(2 * (t - 1) + 1))

    _flash_pipeline(n, heads, first_score, score, vt_tile,
                    lambda t: jnp.where(jnp.logical_and(t > 0, near(jnp.maximum(t, 1))), -NEG, 0.0), *flash_refs)
    m_ref, acc_ref = flash_refs[-2:]

    def redo(t, carry):
        @pl.when(near(t))
        def _():
            posk = jnp.broadcast_to(posk_ref[t - 1], (LANES, P)).T[:, 0:1]
            bucket = _rel_bucket(posq_ref[0] - posk)
            for g in range(heads):
                s = score(t, g) + _bias_lookup(bucket, tbl_ref[pl.ds(hg * heads + g, 1), :])
                p, alpha = _flash_softmax(s, m_ref, g)
                _flash_accumulate(vt_tile(t, g), p, alpha, acc_ref, g)
        return carry

    lax.fori_loop(1, n + 1, redo, 0)
    for g in range(heads):
        o_ref[:, g * LANES:(g + 1) * LANES] = _flash_out(acc_ref, g).astype(o_ref.dtype)


def moba_attention(q, k, vt, kmean_p, bias_t, pos_blk, tbl, qmin, kmax, heads=4):
    S, W = q.shape
    L = MOBA_BLOCK
    NB = S // L
    assert NB % 2 == 0
    gw = heads * LANES
    qspec = pl.BlockSpec((L, gw), lambda h, i, *_: (i, h))
    pos_pair = pos_blk.reshape(NB // 2, 1, 2 * L)
    grid_spec = pltpu.PrefetchScalarGridSpec(
        num_scalar_prefetch=2,
        grid=(W // gw, NB),
        in_specs=[
            qspec,
            pl.BlockSpec((S, gw), lambda h, i, *_: (0, h)),
            pl.BlockSpec((heads * VT_ROWS, S), lambda h, i, *_: (h, 0)),
            pl.BlockSpec((kmean_p.shape[0], gw), lambda h, i, *_: (0, h)),
            pl.BlockSpec((1, heads, 2 * L, L), lambda h, i, *_: (i, h, 0, 0)),
            pl.BlockSpec((1, 1, L), lambda h, i, *_: (i, 0, 0)),
            pl.BlockSpec(pos_pair.shape, lambda h, i, *_: (0, 0, 0)),
            pl.BlockSpec(tbl.shape, lambda h, i, *_: (0, 0)),
        ],
        out_specs=qspec,
        scratch_shapes=[pltpu.VMEM((heads, L, 2 * LANES), BF16)] * 2 + _flash_scratch(heads, 2 * L, L),
    )
    return pl.pallas_call(
        functools.partial(_moba_kernel, heads=heads, nb=NB),
        grid_spec=grid_spec,
        out_shape=jax.ShapeDtypeStruct((S, W), BF16),
        compiler_params=_params("parallel", "arbitrary"),
        name="moba_attention",
    )(qmin, kmax, q, k, vt, kmean_p, bias_t, pos_blk.reshape(NB, 1, L), pos_pair, tbl)


def _rope_lanes(w):
    half = QK_ROPE // 2
    z = jnp.zeros(w.shape[:-1] + (half,), w.dtype)
    return jnp.concatenate([w[..., :half], z, w[..., half:], z], axis=-1)


def _prep_mla(w_in, w_q_up, w_kv_up):
    w1 = jnp.concatenate([w_in[:, :Q_LORA + KV_LORA], _rope_lanes(w_in[:, Q_LORA + KV_LORA:])], axis=1)
    wq = w_q_up.reshape(Q_LORA, MLA_HEADS, QK_NOPE + QK_ROPE)
    wqn = wq[:, :, :QK_NOPE].reshape(Q_LORA, MLA_HEADS * QK_NOPE)
    wqr = _rope_lanes(wq[:, :, QK_NOPE:]).reshape(Q_LORA, MLA_HEADS * LANES)
    wkv = w_kv_up.reshape(KV_LORA, MLA_HEADS, QK_NOPE + V_HEAD)
    wkn = wkv[:, :, :QK_NOPE].reshape(KV_LORA, -1)
    wvt = wkv[:, :, QK_NOPE:].reshape(KV_LORA, -1).T
    return [w.astype(BF16) for w in (w1, wqn, wqr, wkn, wvt)]


def kernel(x, positions, norm_gains, a_w_in, a_q_norm, a_w_q_up, a_kv_norm, a_w_kv_up, a_w_o, b_kv_norm, b_w_kv,
           b_w_q, b_w_o, rel_bias, ffn_w_in, ffn_conv_w, ffn_conv_b, ffn_w_out):
    B, S, D = x.shape
    depth = norm_gains.shape[0]
    n_a = a_w_in.shape[0]
    L = MOBA_BLOCK
    NB = S // L
    HW = MOBA_HEADS * MOBA_HEAD
    ffn_w_in_b = ffn_w_in.astype(BF16)
    ffn_w_out_b = ffn_w_out.astype(BF16)
    ffn_conv_b3 = ffn_conv_b.reshape(depth, 1, -1)
    outs = []
    for b in range(B):
        pos = positions[b]
        pos_col = pos.reshape(S, 1)
        pos_blk = pos.reshape(NB, L)
        qmin = jnp.min(pos_blk, axis=1)
        kmax = jnp.max(pos_blk, axis=1)
        cos_t, sin_t = rope_tables(pos_col)
        tbl = (rel_bias - rel_bias[REL_BUCKETS - 1:REL_BUCKETS, :]).T * LOG2E
        tbl = jnp.pad(tbl, ((0, 0), (0, LANES - REL_BUCKETS)))
        bias_t = moba_bias_tiles(pos_blk, pos_col, tbl) if depth > n_a else None

        h = x[b]
        xn = norm_cast(h, norm_gains[0, 0])
        k = vt = kmean_p = None
        for layer in range(depth):
            g = norm_gains[layer]
            if layer < n_a:
                w1, wqn, wqr, wkn, wvt = _prep_mla(a_w_in[layer], a_w_q_up[layer], a_w_kv_up[layer])
                q_scale = (QK_NOPE + QK_ROPE) ** -0.5 * LOG2E
                qn, qr, kn, kr, vt_a = mla_proj(xn, w1, a_q_norm[layer].reshape(1, -1),
                                                a_kv_norm[layer].reshape(1, -1), wqn, wqr, wkn, wvt,
                                                cos_t, sin_t, q_scale)
                o = mla_attention(qn, qr, kn, kr, vt_a)
                w_o = a_w_o[layer]
            else:
                j = layer - n_a
                q = matmul(xn, b_w_q[j].astype(BF16), scale=MOBA_HEAD ** -0.5 * LOG2E)
                o = moba_attention(q, k, vt, kmean_p, bias_t, pos_blk, tbl, qmin, kmax)
                w_o = b_w_o[j]
            h, xn = proj_res_norm(o, w_o.astype(BF16), h, g[1], g[2])
            if layer + 1 == depth:
                g_next = None
            elif layer + 1 == n_a:
                g_next = jnp.stack([norm_gains[layer + 1, 0], b_kv_norm])
            else:
                g_next = norm_gains[layer + 1, 0].reshape(1, D)
            h, nxt = ffn(xn, layer, ffn_w_in_b, ffn_conv_w, ffn_conv_b3, ffn_w_out_b, h, g[3], g_next)
            if nxt:
                xn = nxt[0]
            if layer + 1 == n_a:
                k, kmean = matmul(nxt[1], b_w_kv[:, :HW].astype(BF16), tm=L, with_mean=True)
                vt = matmul_nt(b_w_kv[:, HW:].T.astype(BF16), nxt[1], ones_rows=True)
                kmean_p = jnp.pad(kmean.reshape(NB, HW), ((0, LANES - NB), (0, 0)))
        outs.append(h)
    return outs[0][None] if B == 1 else jnp.stack(outs)
```

```python
import functools
import math

import jax
import jax.numpy as jnp
from jax import lax
from jax.experimental import pallas as pl
from jax.experimental.pallas import tpu as pltpu

F32 = jnp.float32
BF16 = jnp.bfloat16

MLA_HEADS = 16
Q_LORA = 512
KV_LORA = 512
QK_NOPE = 128
QK_ROPE = 64
V_HEAD = 128
ROPE_THETA = 10000.0
MOBA_HEADS = 16
MOBA_HEAD = 128
MOBA_BLOCK = 256
MOBA_TOPK = 3
REL_BUCKETS = 32
REL_MAX_DIST = 128
CONV_WIDTH = 3
EPS = 1e-6
NEG = -1e30
LOG2E = 1.4426950408889634

LANES = 128
FAR_DIST = REL_MAX_DIST
VMEM_LIMIT = 56 * 1024 * 1024


def _params(*sem):
    return pltpu.CompilerParams(dimension_semantics=sem, vmem_limit_bytes=VMEM_LIMIT)


def _rms_scale(x):
    return lax.rsqrt(jnp.mean(x * x, axis=-1, keepdims=True) + EPS)


def _dot(a, b):
    return jnp.dot(a, b, preferred_element_type=F32)


def _dot_nt(a, b):
    return lax.dot_general(a, b, (((1,), (1,)), ((), ())), preferred_element_type=F32)


def _norm_kernel(x_ref, g_ref, o_ref):
    x = x_ref[...]
    o_ref[...] = (x * _rms_scale(x) * g_ref[...]).astype(o_ref.dtype)


def norm_cast(x, g, tm=512):
    S, D = x.shape
    tm = min(tm, S)
    return pl.pallas_call(
        _norm_kernel,
        grid=(S // tm,),
        in_specs=[pl.BlockSpec((tm, D), lambda i: (i, 0)), pl.BlockSpec((1, D), lambda i: (0, 0))],
        out_specs=pl.BlockSpec((tm, D), lambda i: (i, 0)),
        out_shape=jax.ShapeDtypeStruct((S, D), BF16),
        compiler_params=_params("parallel"),
        name="norm_cast",
    )(x, g.reshape(1, D))


def _rope_table_kernel(pos_ref, inv_ref, sgn_ref, cos_ref, sin_ref):
    ang = pos_ref[...].astype(F32) * inv_ref[...]
    cos_ref[...] = jnp.cos(ang)
    sin_ref[...] = jnp.sin(ang) * sgn_ref[...]


def rope_tables(pos_col, tm=1024):
    S = pos_col.shape[0]
    tm = min(tm, S)
    half = QK_ROPE // 2
    inv = ROPE_THETA ** (-jnp.arange(half, dtype=F32) / half)
    z = jnp.zeros((half,), F32)
    inv_pat = jnp.concatenate([inv, z, inv, z]).reshape(1, LANES)
    o = jnp.ones((2 * half,), F32)
    sgn = jnp.concatenate([-o, o]).reshape(1, LANES)
    return pl.pallas_call(
        _rope_table_kernel,
        grid=(S // tm,),
        in_specs=[pl.BlockSpec((tm, 1), lambda i: (i, 0)),
                  pl.BlockSpec((1, LANES), lambda i: (0, 0)),
                  pl.BlockSpec((1, LANES), lambda i: (0, 0))],
        out_specs=[pl.BlockSpec((tm, LANES), lambda i: (i, 0))] * 2,
        out_shape=[jax.ShapeDtypeStruct((S, LANES), F32)] * 2,
        compiler_params=_params("parallel"),
        name="rope_tables",
    )(pos_col, inv_pat, sgn)


def _mla_proj_kernel(xn_ref, w1_ref, gq_ref, gkv_ref, wqn_ref, wqr_ref, wkn_ref, wvt_ref, cos_ref, sin_ref,
                     qn_ref, qr_ref, kn_ref, kr_ref, vt_ref, *, q_scale):
    xn = xn_ref[...]
    cos = cos_ref[...]
    sin = sin_ref[...]

    def rope(x):
        return x * cos + pltpu.roll(x, LANES // 2, axis=1) * sin

    cq = _dot(xn, w1_ref[:, 0:Q_LORA])
    ckv = _dot(xn, w1_ref[:, Q_LORA:Q_LORA + KV_LORA])
    kr = _dot(xn, w1_ref[:, Q_LORA + KV_LORA:])
    kr_ref[...] = rope(kr).astype(BF16)
    cqn = (cq * _rms_scale(cq) * gq_ref[...]).astype(BF16)
    ckvn = (ckv * _rms_scale(ckv) * gkv_ref[...]).astype(BF16)
    width = MLA_HEADS * QK_NOPE
    chunk = 4 * LANES
    for c in range(width // chunk):
        sl = slice(c * chunk, (c + 1) * chunk)
        qn_ref[:, sl] = (_dot(cqn, wqn_ref[:, sl]) * q_scale).astype(BF16)
        qr = _dot(cqn, wqr_ref[:, sl])
        for hh in range(chunk // LANES):
            x = qr[:, hh * LANES:(hh + 1) * LANES]
            lo = c * chunk + hh * LANES
            qr_ref[:, lo:lo + LANES] = (rope(x) * q_scale).astype(BF16)
        kn_ref[:, sl] = _dot(ckvn, wkn_ref[:, sl]).astype(BF16)
        vt = _dot_nt(wvt_ref[sl, :], ckvn).astype(BF16)
        hpc = chunk // LANES
        vt_ref[c * hpc * VT_ROWS:(c + 1) * hpc * VT_ROWS, :] = _with_ones_rows(vt, hpc)


def mla_proj(xn, w1, gq, gkv, wqn, wqr, wkn, wvt, cos_t, sin_t, q_scale, tm=256):
    S, D = xn.shape
    tm = min(tm, S)
    W = MLA_HEADS * QK_NOPE
    full = lambda a: pl.BlockSpec(a.shape, lambda i: (0, 0))
    row = lambda n: pl.BlockSpec((tm, n), lambda i: (i, 0))
    return pl.pallas_call(
        functools.partial(_mla_proj_kernel, q_scale=q_scale),
        grid=(S // tm,),
        in_specs=[row(D), full(w1), full(gq), full(gkv), full(wqn), full(wqr), full(wkn), full(wvt),
                  row(LANES), row(LANES)],
        out_specs=[row(W), row(W), row(W), row(LANES), pl.BlockSpec((MLA_HEADS * VT_ROWS, tm), lambda i: (0, i))],
        out_shape=[jax.ShapeDtypeStruct((S, W), BF16), jax.ShapeDtypeStruct((S, W), BF16),
                   jax.ShapeDtypeStruct((S, W), BF16), jax.ShapeDtypeStruct((S, LANES), BF16),
                   jax.ShapeDtypeStruct((MLA_HEADS * VT_ROWS, S), BF16)],
        compiler_params=_params("parallel"),
        name="mla_proj",
    )(xn, w1, gq, gkv, wqn, wqr, wkn, wvt, cos_t, sin_t)


ONES_ROWS = 16
VT_ROWS = LANES + ONES_ROWS


def _flash_init(m_ref, acc_ref, g):
    m_ref[g] = jnp.full(m_ref.shape[1:], NEG, F32)
    acc_ref[g] = jnp.zeros(acc_ref.shape[1:], F32)


def _flash_softmax(st, m_ref, g, off=None):
    m_prev = m_ref[g]
    hq = st.shape[1] // 2
    ps, ms = [], []
    for cs in (slice(0, hq), slice(hq, 2 * hq)):
        sh = st[:, cs]
        mh = jnp.maximum(m_prev[:, cs], jnp.max(sh, axis=0, keepdims=True))
        ps.append(jnp.exp2(sh - (mh if off is None else mh + off)).astype(BF16))
        ms.append(mh)
    m_new = jnp.concatenate(ms, axis=1)
    alpha = jnp.exp2(m_prev - m_new)
    m_ref[g] = m_new
    return jnp.concatenate(ps, axis=1), alpha


def _flash_accumulate(vt, p, alpha, acc_ref, g):
    acc_ref[g] = alpha * acc_ref[g] + _dot(vt, p)


def _flash_out(acc_ref, g):
    acc = acc_ref[g]
    return (acc[0:LANES] / acc[LANES:LANES + 1]).T


def _with_ones_rows(vt, heads):
    ones = jnp.ones((ONES_ROWS, vt.shape[1]), vt.dtype)
    parts = []
    for h in range(heads):
        parts += [vt[h * LANES:(h + 1) * LANES], ones]
    return jnp.concatenate(parts, axis=0)


def _flash_pipeline(n, heads, first_score, score, vt_tile, off, s_ref, p_ref, a_ref, m_ref, acc_ref):
    for g in range(heads):
        _flash_init(m_ref, acc_ref, g)
        p_ref[g, 1] = jnp.zeros(p_ref.shape[2:], BF16)
        a_ref[g, 1] = jnp.ones(a_ref.shape[2:], F32)
        s_ref[g, 0] = first_score(g)

    def tick(t, a, b):
        for g in range(heads):
            s_ref[g, b] = score(t + 1, g)
        tp = jnp.maximum(t - 1, 0)
        for g in range(heads):
            _flash_accumulate(vt_tile(tp, g), p_ref[g, b], a_ref[g, b], acc_ref, g)
        o = None if off is None else off(t)
        for g in range(heads):
            p, alpha = _flash_softmax(s_ref[g, a], m_ref, g, o)
            p_ref[g, a] = p
            a_ref[g, a] = alpha

    def body(k, carry):
        tick(2 * k, 0, 1)
        tick(2 * k + 1, 1, 0)
        return carry

    lax.fori_loop(0, n // 2, body, 0)

    @pl.when(n % 2 == 1)
    def _():
        tick(n - 1, 0, 1)

    def last(e):
        tp = jnp.maximum(n - 1, 0)
        for g in range(heads):
            _flash_accumulate(vt_tile(tp, g), p_ref[g, 1 - e], a_ref[g, 1 - e], acc_ref, g)
        o = None if off is None else off(n)
        for g in range(heads):
            p, alpha = _flash_softmax(s_ref[g, e], m_ref, g, o)
            _flash_accumulate(vt_tile(n, g), p, alpha, acc_ref, g)

    @pl.when(n % 2 == 0)
    def _():
        last(0)

    @pl.when(n % 2 == 1)
    def _():
        last(1)


def _flash_scratch(heads, tk, tq):
    return [pltpu.VMEM((heads, 2, tk, tq), F32), pltpu.VMEM((heads, 2, tk, tq), BF16),
            pltpu.VMEM((heads, 2, 1, tq), F32), pltpu.VMEM((heads, 1, tq), F32),
            pltpu.VMEM((heads, VT_ROWS, tq), F32)]


def _mla_attn_kernel(qn_ref, qr_ref, kn_ref, kr_ref, vt_ref, o_ref, q_ref, *flash_refs, tq, heads):
    i = pl.program_id(1)
    for g in range(heads):
        hs = slice(g * LANES, (g + 1) * LANES)
        q_ref[g, :, 0:LANES] = qn_ref[:, hs]
        q_ref[g, :, LANES:] = qr_ref[:, hs]

    def rows(t):
        return pl.ds(pl.multiple_of(jnp.where(t == 0, i, t - 1) * tq, tq), tq)

    def score(t, g):
        k = jnp.concatenate([kn_ref[rows(t), g * LANES:(g + 1) * LANES], kr_ref[rows(t), :]], axis=1)
        return _dot_nt(k, q_ref[g])

    def vt_tile(t, g):
        return vt_ref[g * VT_ROWS:(g + 1) * VT_ROWS, rows(t)]

    def diagonal(g):
        kidx = lax.broadcasted_iota(jnp.int32, (tq, tq), 0)
        qidx = lax.broadcasted_iota(jnp.int32, (tq, tq), 1)
        return jnp.where(kidx <= qidx, score(0, g), NEG)

    _flash_pipeline(i, heads, diagonal, score, vt_tile, None, *flash_refs)
    acc_ref = flash_refs[-1]
    for g in range(heads):
        o_ref[:, g * LANES:(g + 1) * LANES] = _flash_out(acc_ref, g).astype(o_ref.dtype)


def mla_attention(qn, qr, kn, kr, vt, tq=512, heads=2):
    S, W = qn.shape
    tq = min(tq, S)
    gw = heads * LANES
    qspec = pl.BlockSpec((tq, gw), lambda h, i: (i, h))
    return pl.pallas_call(
        functools.partial(_mla_attn_kernel, tq=tq, heads=heads),
        grid=(W // gw, S // tq),
        in_specs=[qspec, qspec,
                  pl.BlockSpec((S, gw), lambda h, i: (0, h)),
                  pl.BlockSpec((S, LANES), lambda h, i: (0, 0)),
                  pl.BlockSpec((heads * VT_ROWS, S), lambda h, i: (h, 0))],
        out_specs=qspec,
        out_shape=jax.ShapeDtypeStruct((S, W), BF16),
        scratch_shapes=[pltpu.VMEM((heads, tq, 2 * LANES), BF16)] + _flash_scratch(heads, tq, tq),
        compiler_params=_params("parallel", "arbitrary"),
        name="mla_attention",
    )(qn, qr, kn, kr, vt)


def _proj_res_kernel(o_ref, w_ref, h_ref, gpost_ref, gnext_ref, hn_ref, xn_ref, mix_ref):
    n = w_ref.shape[1]
    chunk = 4 * LANES
    th = o_ref.shape[0] // 2
    for r in range(2):
        rows = pl.ds(r * th, th)
        o = o_ref[rows, :]
        for c in range(n // chunk):
            sl = slice(c * chunk, (c + 1) * chunk)
            mix_ref[rows, sl] = _dot(o, w_ref[:, sl])
        mix = mix_ref[rows, :]
        hn = h_ref[rows, :] + mix * _rms_scale(mix) * gpost_ref[...]
        hn_ref[rows, :] = hn
        xn_ref[rows, :] = (hn * _rms_scale(hn) * gnext_ref[...]).astype(xn_ref.dtype)


def proj_res_norm(o, w, h, g_post, g_next, tm=512):
    S, K = o.shape
    D = w.shape[1]
    tm = min(tm, S)
    row = lambda n: pl.BlockSpec((tm, n), lambda i: (i, 0))
    gspec = pl.BlockSpec((1, D), lambda i: (0, 0))
    return pl.pallas_call(
        _proj_res_kernel,
        grid=(S // tm,),
        in_specs=[row(K), pl.BlockSpec((K, D), lambda i: (0, 0)), row(D), gspec, gspec],
        out_specs=[row(D), row(D)],
        out_shape=[jax.ShapeDtypeStruct((S, D), F32), jax.ShapeDtypeStruct((S, D), BF16)],
        scratch_shapes=[pltpu.VMEM((tm, D), F32)],
        compiler_params=_params("parallel"),
        name="proj_res_norm",
    )(o, w, h, g_post.reshape(1, D), g_next.reshape(1, D))


CARRY = 8


def _gelu_tanh(x):
    return 0.5 * x * (1.0 + jnp.tanh(math.sqrt(2.0 / math.pi) * (x + 0.044715 * (x * x * x))))


def _ffn_kernel(xn_ref, wg_ref, wu_ref, cwg_ref, cwu_ref, cbg_ref, cbu_ref, wo_ref, h_ref,
                gpost_ref, gnext_ref, hn_ref, *rest, n_next):
    xn_next_refs = rest[:n_next]
    yg_ref, yu_ref, cg_ref, cu_ref, acc_ref = rest[n_next:]
    i = pl.program_id(0)
    c = pl.program_id(1)
    tm = xn_ref.shape[0]
    th = tm // 2

    @pl.when(c == 0)
    def _():
        acc_ref[...] = jnp.zeros(acc_ref.shape, F32)

    for y_ref, carry_ref in ((yg_ref, cg_ref), (yu_ref, cu_ref)):
        prev = carry_ref[c]
        y_ref[0:CARRY, :] = jnp.where(i == 0, jnp.zeros_like(prev), prev)

    def conv(r, w_ref, y_ref, cw_ref, cb_ref):
        base = CARRY + r * th
        y_ref[pl.ds(base, th), :] = _dot(xn_ref[pl.ds(r * th, th), :], w_ref[...])
        cw = cw_ref[...]
        out = (y_ref[pl.ds(base - 2, th), :] * cw[0:1] + y_ref[pl.ds(base - 1, th), :] * cw[1:2]
               + y_ref[pl.ds(base, th), :] * cw[2:3])
        return out + cb_ref[...]

    acts = []
    for r in range(2):
        act = _gelu_tanh(conv(r, wg_ref, yg_ref, cwg_ref, cbg_ref)) * conv(r, wu_ref, yu_ref, cwu_ref, cbu_ref)
        acts.append(act.astype(BF16))
    cg_ref[c] = yg_ref[pl.ds(tm, CARRY), :]
    cu_ref[c] = yu_ref[pl.ds(tm, CARRY), :]
    for r in range(2):
        acc_ref[pl.ds(r * th, th), :] += _dot(acts[r], wo_ref[...])

    @pl.when(c == pl.num_programs(1) - 1)
    def _():
        f = acc_ref[...]
        hn = h_ref[...] + f * _rms_scale(f) * gpost_ref[...]
        hn_ref[...] = hn
        if n_next:
            y = hn * _rms_scale(hn)
            for k in range(n_next):
                xn_next_refs[k][...] = (y * gnext_ref[k:k + 1, :]).astype(BF16)


def ffn(xn, layer, w_in, conv_w, conv_b, w_out, h, g_post, g_next, tm=512, tf=512):
    S, D = xn.shape
    FF = w_out.shape[1]
    tm = min(tm, S)
    nf = FF // tf
    n_next = 0 if g_next is None else g_next.shape[0]
    if g_next is None:
        g_next = jnp.ones((1, D), F32)
    cb = conv_b
    row = pl.BlockSpec((tm, D), lambda i, c: (i, 0))
    in_specs = [
        row,
        pl.BlockSpec((None, D, tf), lambda i, c: (layer, 0, c)),
        pl.BlockSpec((None, D, tf), lambda i, c: (layer, 0, c + nf)),
        pl.BlockSpec((None, CONV_WIDTH, tf), lambda i, c: (layer, 0, c)),
        pl.BlockSpec((None, CONV_WIDTH, tf), lambda i, c: (layer, 0, c + nf)),
        pl.BlockSpec((None, 1, tf), lambda i, c: (layer, 0, c)),
        pl.BlockSpec((None, 1, tf), lambda i, c: (layer, 0, c + nf)),
        pl.BlockSpec((None, tf, D), lambda i, c: (layer, c, 0)),
        row,
        pl.BlockSpec((1, D), lambda i, c: (0, 0)),
        pl.BlockSpec(g_next.shape, lambda i, c: (0, 0)),
    ]
    outs = pl.pallas_call(
        functools.partial(_ffn_kernel, n_next=n_next),
        grid=(S // tm, nf),
        in_specs=in_specs,
        out_specs=[row] * (1 + n_next),
        out_shape=[jax.ShapeDtypeStruct((S, D), F32)] + [jax.ShapeDtypeStruct((S, D), BF16)] * n_next,
        scratch_shapes=[pltpu.VMEM((CARRY + tm, tf), F32), pltpu.VMEM((CARRY + tm, tf), F32),
                        pltpu.VMEM((nf, CARRY, tf), F32), pltpu.VMEM((nf, CARRY, tf), F32),
                        pltpu.VMEM((tm, D), F32)],
        compiler_params=_params("arbitrary", "arbitrary"),
        name="conv_glu_ffn",
    )(xn, w_in, w_in, conv_w, conv_w, cb, cb, w_out, h, g_post.reshape(1, D), g_next)
    return outs[0], list(outs[1:])


def _matmul_kernel(x_ref, w_ref, o_ref, *mean_ref, scale):
    y = _dot(x_ref[...], w_ref[...])
    if scale != 1.0:
        y = y * scale
    o_ref[...] = y.astype(o_ref.dtype)
    if mean_ref:
        mean_ref[0][...] = jnp.mean(y, axis=0, keepdims=True)[None]


def matmul(x, w, scale=1.0, tm=512, tn=1024, with_mean=False):
    S, K = x.shape
    N = w.shape[1]
    tm = min(tm, S)
    out_specs = [pl.BlockSpec((tm, tn), lambda n, i: (i, n))]
    out_shape = [jax.ShapeDtypeStruct((S, N), BF16)]
    if with_mean:
        out_specs.append(pl.BlockSpec((1, 1, tn), lambda n, i: (i, 0, n)))
        out_shape.append(jax.ShapeDtypeStruct((S // tm, 1, N), F32))
    outs = pl.pallas_call(
        functools.partial(_matmul_kernel, scale=scale),
        grid=(N // tn, S // tm),
        in_specs=[pl.BlockSpec((tm, K), lambda n, i: (i, 0)), pl.BlockSpec((K, tn), lambda n, i: (0, n))],
        out_specs=out_specs,
        out_shape=out_shape,
        compiler_params=_params("parallel", "parallel"),
        name="matmul_mean" if with_mean else "matmul",
    )(x, w)
    return outs if with_mean else outs[0]


def _matmul_nt_kernel(a_ref, b_ref, o_ref, *, heads):
    y = _dot_nt(a_ref[...], b_ref[...]).astype(o_ref.dtype)
    o_ref[...] = _with_ones_rows(y, heads) if heads else y


def matmul_nt(a, b, tm=1024, tn=512, ones_rows=False):
    M, K = a.shape
    N = b.shape[0]
    tm, tn = min(tm, M), min(tn, N)
    heads = tm // LANES if ones_rows else 0
    to = heads * VT_ROWS if ones_rows else tm
    return pl.pallas_call(
        functools.partial(_matmul_nt_kernel, heads=heads),
        grid=(M // tm, N // tn),
        in_specs=[pl.BlockSpec((tm, K), lambda m, n: (m, 0)), pl.BlockSpec((tn, K), lambda m, n: (n, 0))],
        out_specs=pl.BlockSpec((to, tn), lambda m, n: (m, n)),
        out_shape=jax.ShapeDtypeStruct((M // tm * to, N), BF16),
        compiler_params=_params("parallel", "parallel"),
        name="matmul_nt",
    )(a, b)


def _rel_bucket(dist):
    n = jnp.maximum(dist, 0)
    max_exact = REL_BUCKETS // 2
    nf = jnp.maximum(n, 1).astype(F32)
    large = max_exact + (jnp.log(nf / max_exact) / math.log(REL_MAX_DIST / max_exact)
                         * (REL_BUCKETS - max_exact)).astype(jnp.int32)
    large = jnp.minimum(large, REL_BUCKETS - 1)
    return jnp.where(n < max_exact, n, large)


def _bias_lookup(bucket, tbl_row):
    rows, cols = bucket.shape
    tb = jnp.broadcast_to(tbl_row, (rows, LANES))
    parts = [jnp.take_along_axis(tb, bucket[:, c * LANES:(c + 1) * LANES], axis=1) for c in range(cols // LANES)]
    return jnp.concatenate(parts, axis=1)


def _moba_bias_kernel(posq_ref, pk0_ref, pk1_ref, tbl_ref, o_ref):
    L = MOBA_BLOCK
    posq = posq_ref[0]
    bucket = _rel_bucket(posq - pk0_ref[...])
    for h in range(o_ref.shape[1]):
        o_ref[0, h, 0:L, :] = _bias_lookup(bucket, tbl_ref[h:h + 1, :])
    bucket = _rel_bucket(posq - pk1_ref[...])
    for h in range(o_ref.shape[1]):
        for j in range(L // LANES):
            live = (j + 1) * LANES
            cols = slice(j * LANES, (j + 1) * LANES)
            o_ref[0, h, L:L + live, cols] = _bias_lookup(bucket[0:live, cols], tbl_ref[h:h + 1, :])
            if live < L:
                o_ref[0, h, L + live:2 * L, cols] = jnp.zeros((L - live, LANES), F32)


def moba_bias_tiles(pos_blk, pos_col, tbl):
    NB, L = pos_blk.shape
    H = tbl.shape[0]
    near0 = lambda i: jnp.maximum(i - 1, 0)
    return pl.pallas_call(
        _moba_bias_kernel,
        grid=(NB,),
        in_specs=[pl.BlockSpec((1, 1, L), lambda i: (i, 0, 0)),
                  pl.BlockSpec((L, 1), lambda i: (near0(i), 0)),
                  pl.BlockSpec((L, 1), lambda i: (near0(i) + 1, 0)),
                  pl.BlockSpec(tbl.shape, lambda i: (0, 0))],
        out_specs=pl.BlockSpec((1, H, 2 * L, L), lambda i: (i, 0, 0, 0)),
        out_shape=jax.ShapeDtypeStruct((NB, H, 2 * L, L), F32),
        compiler_params=_params("parallel"),
        name="moba_bias_tiles",
    )(pos_blk.reshape(NB, 1, L), pos_col, pos_col, tbl)


def _moba_kernel(qmin_ref, kmax_ref, q_ref, k_ref, vt_ref, kmean_ref, bias_ref, posq_ref, posk_ref, tbl_ref,
                 o_ref, qaug_ref, qnear_ref, *flash_refs, heads, nb):
    hg = pl.program_id(0)
    i = pl.program_id(1)
    L = MOBA_BLOCK
    P = 2 * L
    n = i // 2
    b0 = jnp.maximum(i - 1, 0)
    nbp = min(LANES, -(-nb // 8) * 8)
    blk = lax.broadcasted_iota(jnp.int32, (nbp, L), 0)
    for g in range(heads):
        hs = slice(g * LANES, (g + 1) * LANES)
        q = q_ref[:, hs]
        gate = _dot_nt(kmean_ref[:, hs].astype(BF16), q)[:nbp]
        gate = jnp.where(blk < i, gate, -jnp.inf)
        keep = jnp.zeros((nbp, L), F32)
        for _ in range(MOBA_TOPK):
            mx = jnp.max(gate, axis=0, keepdims=True)
            first = jnp.min(jnp.where(gate == mx, blk, nbp), axis=0, keepdims=True)
            hit = blk == first
            keep = jnp.where(hit, 1.0, keep)
            gate = jnp.where(hit, -jnp.inf, gate)
        keep = jnp.where(blk < i, keep, jnp.where(blk == i, 1.0, 0.0))
        pen = jnp.where(keep > 0.0, 0.0, NEG)
        if nbp < LANES:
            pen = jnp.concatenate([pen, jnp.zeros((LANES - nbp, L), F32)], axis=0)
        pen_t = pen.T
        blk_t = lax.broadcasted_iota(jnp.int32, (L, LANES), 1)
        pen_sweep_t = jnp.where(blk_t >= b0, NEG, pen_t)
        qnear_ref[g, :, 0:LANES] = q
        qnear_ref[g, :, LANES:] = pen_t.astype(BF16)
        qaug_ref[g, :, 0:LANES] = q
        qaug_ref[g, :, LANES:] = pen_sweep_t.astype(BF16)

    lane = lax.broadcasted_iota(jnp.int32, (P, LANES), 1)
    half = jnp.where(lax.broadcasted_iota(jnp.int32, (P, LANES), 0) >= L, 1, 0)

    def rows(t):
        return pl.ds(pl.multiple_of(jnp.where(t == 0, b0 * L, (t - 1) * P), L), P)

    def raw_score(t, g, first_blk, qa_ref):
        onehot = jnp.where(lane == first_blk + half, 1.0, 0.0).astype(BF16)
        kaug = jnp.concatenate([k_ref[rows(t), g * LANES:(g + 1) * LANES], onehot], axis=1)
        return _dot_nt(kaug, qa_ref[g])

    def score(t, g):
        return raw_score(t, g, 2 * (t - 1), qaug_ref)

    def vt_tile(t, g):
        return vt_ref[g * VT_ROWS:(g + 1) * VT_ROWS, rows(t)]

    def first_score(g):
        kidx = b0 * L + lax.broadcasted_iota(jnp.int32, (P, L), 0)
        qidx = i * L + lax.broadcasted_iota(jnp.int32, (P, L), 1)
        return jnp.where(kidx <= qidx, raw_score(0, g, b0, qnear_ref) + bias_ref[0, g], NEG)

    def near(t):
        def blk_near(b):
            return jnp.logical_and(b < b0, qmin_ref[i] - kmax_ref[b] < FAR_DIST)
        return jnp.logical_or(blk_near(2 * (t - 1)), blk_near(2 * (t - 1) + 1))

    _flash_pipeline(n, heads, first_score, score, vt_tile,
                    lambda t: jnp.where(jnp.logical_and(t > 0, near(jnp.maximum(t, 1))), -NEG, 0.0), *flash_refs)
    m_ref, acc_ref = flash_refs[-2:]

    def redo(t, carry):
        @pl.when(near(t))
        def _():
            posk = jnp.broadcast_to(posk_ref[t - 1], (LANES, P)).T[:, 0:1]
            bucket = _rel_bucket(posq_ref[0] - posk)
            for g in range(heads):
                s = score(t, g) + _bias_lookup(bucket, tbl_ref[pl.ds(hg * heads + g, 1), :])
                p, alpha = _flash_softmax(s, m_ref, g)
                _flash_accumulate(vt_tile(t, g), p, alpha, acc_ref, g)
        return carry

    lax.fori_loop(1, n + 1, redo, 0)
    for g in range(heads):
        o_ref[:, g * LANES:(g + 1) * LANES] = _flash_out(acc_ref, g).astype(o_ref.dtype)


def moba_attention(q, k, vt, kmean_p, bias_t, pos_blk, tbl, qmin, kmax, heads=4):
    S, W = q.shape
    L = MOBA_BLOCK
    NB = S // L
    assert NB % 2 == 0
    gw = heads * LANES
    qspec = pl.BlockSpec((L, gw), lambda h, i, *_: (i, h))
    pos_pair = pos_blk.reshape(NB // 2, 1, 2 * L)
    grid_spec = pltpu.PrefetchScalarGridSpec(
        num_scalar_prefetch=2,
        grid=(W // gw, NB),
        in_specs=[
            qspec,
            pl.BlockSpec((S, gw), lambda h, i, *_: (0, h)),
            pl.BlockSpec((heads * VT_ROWS, S), lambda h, i, *_: (h, 0)),
            pl.BlockSpec((kmean_p.shape[0], gw), lambda h, i, *_: (0, h)),
            pl.BlockSpec((1, heads, 2 * L, L), lambda h, i, *_: (i, h, 0, 0)),
            pl.BlockSpec((1, 1, L), lambda h, i, *_: (i, 0, 0)),
            pl.BlockSpec(pos_pair.shape, lambda h, i, *_: (0, 0, 0)),
            pl.BlockSpec(tbl.shape, lambda h, i, *_: (0, 0)),
        ],
        out_specs=qspec,
        scratch_shapes=[pltpu.VMEM((heads, L, 2 * LANES), BF16)] * 2 + _flash_scratch(heads, 2 * L, L),
    )
    return pl.pallas_call(
        functools.partial(_moba_kernel, heads=heads, nb=NB),
        grid_spec=grid_spec,
        out_shape=jax.ShapeDtypeStruct((S, W), BF16),
        compiler_params=_params("parallel", "arbitrary"),
        name="moba_attention",
    )(qmin, kmax, q, k, vt, kmean_p, bias_t, pos_blk.reshape(NB, 1, L), pos_pair, tbl)


def _rope_lanes(w):
    half = QK_ROPE // 2
    z = jnp.zeros(w.shape[:-1] + (half,), w.dtype)
    return jnp.concatenate([w[..., :half], z, w[..., half:], z], axis=-1)


def _prep_mla(w_in, w_q_up, w_kv_up):
    w1 = jnp.concatenate([w_in[:, :Q_LORA + KV_LORA], _rope_lanes(w_in[:, Q_LORA + KV_LORA:])], axis=1)
    wq = w_q_up.reshape(Q_LORA, MLA_HEADS, QK_NOPE + QK_ROPE)
    wqn = wq[:, :, :QK_NOPE].reshape(Q_LORA, MLA_HEADS * QK_NOPE)
    wqr = _rope_lanes(wq[:, :, QK_NOPE:]).reshape(Q_LORA, MLA_HEADS * LANES)
    wkv = w_kv_up.reshape(KV_LORA, MLA_HEADS, QK_NOPE + V_HEAD)
    wkn = wkv[:, :, :QK_NOPE].reshape(KV_LORA, -1)
    wvt = wkv[:, :, QK_NOPE:].reshape(KV_LORA, -1).T
    return [w.astype(BF16) for w in (w1, wqn, wqr, wkn, wvt)]


def kernel(x, positions, norm_gains, a_w_in, a_q_norm, a_w_q_up, a_kv_norm, a_w_kv_up, a_w_o, b_kv_norm, b_w_kv,
           b_w_q, b_w_o, rel_bias, ffn_w_in, ffn_conv_w, ffn_conv_b, ffn_w_out):
    B, S, D = x.shape
    depth = norm_gains.shape[0]
    n_a = a_w_in.shape[0]
    L = MOBA_BLOCK
    NB = S // L
    HW = MOBA_HEADS * MOBA_HEAD
    ffn_w_in_b = ffn_w_in.astype(BF16)
    ffn_w_out_b = ffn_w_out.astype(BF16)
    ffn_conv_b3 = ffn_conv_b.reshape(depth, 1, -1)
    outs = []
    for b in range(B):
        pos = positions[b]
        pos_col = pos.reshape(S, 1)
        pos_blk = pos.reshape(NB, L)
        qmin = jnp.min(pos_blk, axis=1)
        kmax = jnp.max(pos_blk, axis=1)
        cos_t, sin_t = rope_tables(pos_col)
        tbl = (rel_bias - rel_bias[REL_BUCKETS - 1:REL_BUCKETS, :]).T * LOG2E
        tbl = jnp.pad(tbl, ((0, 0), (0, LANES - REL_BUCKETS)))
        bias_t = moba_bias_tiles(pos_blk, pos_col, tbl) if depth > n_a else None

        h = x[b]
        xn = norm_cast(h, norm_gains[0, 0])
        k = vt = kmean_p = None
        for layer in range(depth):
            g = norm_gains[layer]
            if layer < n_a:
                w1, wqn, wqr, wkn, wvt = _prep_mla(a_w_in[layer], a_w_q_up[layer], a_w_kv_up[layer])
                q_scale = (QK_NOPE + QK_ROPE) ** -0.5 * LOG2E
                qn, qr, kn, kr, vt_a = mla_proj(xn, w1, a_q_norm[layer].reshape(1, -1),
                                                a_kv_norm[layer].reshape(1, -1), wqn, wqr, wkn, wvt,
                                                cos_t, sin_t, q_scale)
                o = mla_attention(qn, qr, kn, kr, vt_a)
                w_o = a_w_o[layer]
            else:
                j = layer - n_a
                q = matmul(xn, b_w_q[j].astype(BF16), scale=MOBA_HEAD ** -0.5 * LOG2E)
                o = moba_attention(q, k, vt, kmean_p, bias_t, pos_blk, tbl, qmin, kmax)
                w_o = b_w_o[j]
            h, xn = proj_res_norm(o, w_o.astype(BF16), h, g[1], g[2])
            if layer + 1 == depth:
                g_next = None
            elif layer + 1 == n_a:
                g_next = jnp.stack([norm_gains[layer + 1, 0], b_kv_norm])
            else:
                g_next = norm_gains[layer + 1, 0].reshape(1, D)
            h, nxt = ffn(xn, layer, ffn_w_in_b, ffn_conv_w, ffn_conv_b3, ffn_w_out_b, h, g[3], g_next)
            if nxt:
                xn = nxt[0]
            if layer + 1 == n_a:
                k, kmean = matmul(nxt[1], b_w_kv[:, :HW].astype(BF16), tm=L, with_mean=True)
                vt = matmul_nt(b_w_kv[:, HW:].T.astype(BF16), nxt[1], ones_rows=True)
                kmean_p = jnp.pad(kmean.reshape(NB, HW), ((0, LANES - NB), (0, 0)))
        outs.append(h)
    return outs[0][None] if B == 1 else jnp.stack(outs)
```
